```python
import math
import jax, jax.numpy as jnp
from jax import lax
import numpy as np

D_MODEL = 1024
BATCH = 8
SEQ = 2048
DEPTH = 4
DEC_BATCH = 32
DEC_SEQ = 1
PAST_LEN = 16384
PAGE_SIZE = 128

SB_HEADS = 8
SB_HEAD_DIM = 64
SB_WIDTH = SB_HEADS * SB_HEAD_DIM
SB_SCALE = 1.0 / math.sqrt(SB_HEAD_DIM)
MLA_HEADS = 8
MLA_NOPE = 64
MLA_ROPE = 32
MLA_V = 64
MLA_WIDTH = MLA_HEADS * MLA_V
Q_RANK = 256
KV_RANK = 128
MLA_SCALE = 1.0 / math.sqrt(MLA_NOPE + MLA_ROPE)
ROPE_THETA = 10000.0
MIX_WIDTH = SB_WIDTH + MLA_WIDTH
IN_COLS = 3 * SB_WIDTH + Q_RANK + KV_RANK + MLA_ROPE
IN_SPLITS = [SB_WIDTH, 2 * SB_WIDTH, 3 * SB_WIDTH, 3 * SB_WIDTH + Q_RANK, 3 * SB_WIDTH + Q_RANK + KV_RANK]
Q_BLOCK = 128
D_FF = 2816
N_EXPERTS = 8
TOP_K = 2
N_DENSE = (DEPTH + 1) // 2
N_MOE = DEPTH // 2
EPS = 1e-6
F32 = jnp.float32

kernel_name = "hybrid_sb_mla_adaln_moe_step"


def rms_norm(x, g):
    xf = x.astype(F32)
    y = xf * lax.rsqrt(jnp.mean(xf * xf, axis=-1, keepdims=True) + EPS)
    return (y * g.astype(F32)).astype(x.dtype)


def rope_tables(pos):
    inv = ROPE_THETA ** (-jnp.arange(0, MLA_ROPE, 2, dtype=F32) / MLA_ROPE)
    ang = pos.astype(F32)[:, None] * inv[None, :]
    return jnp.cos(ang), jnp.sin(ang)


def apply_rope(x, cos, sin):
    x1, x2 = jnp.split(x.astype(F32), 2, axis=-1)
    return jnp.concatenate([x1 * cos - x2 * sin, x1 * sin + x2 * cos], axis=-1).astype(x.dtype)


def ada_mod(c, w_ada, b_ada):
    mod = jax.nn.silu(c) @ w_ada + b_ada
    return jnp.split(mod[:, None, :], 6, axis=-1)


def modulate(x, g, shift, scale):
    return rms_norm(x, g) * (1.0 + scale) + shift


def mixer_inputs(h, w_in, g_q, w_q_up, g_kv, w_uk, pos):
    B, T, _ = h.shape
    sb_q, sb_k, sb_v, q_lat, kv_lat, k_pe = jnp.split(h @ w_in, IN_SPLITS, axis=-1)
    sb_q = sb_q.reshape(B, T, SB_HEADS, SB_HEAD_DIM)
    sb_k = sb_k.reshape(B, T, SB_HEADS, SB_HEAD_DIM)
    sb_v = sb_v.reshape(B, T, SB_HEADS, SB_HEAD_DIM)
    q = (rms_norm(q_lat, g_q) @ w_q_up).reshape(B, T, MLA_HEADS, MLA_NOPE + MLA_ROPE)
    q_nope, q_pe = q[..., :MLA_NOPE], q[..., MLA_NOPE:]
    cos, sin = rope_tables(pos)
    q_pe = apply_rope(q_pe, cos[None, :, None, :], sin[None, :, None, :])
    k_pe = apply_rope(k_pe, cos[None], sin[None])
    ckv = rms_norm(kv_lat, g_kv)
    q_abs = jnp.einsum('bthn,chn->bthc', q_nope, w_uk)
    return sb_q, sb_k, sb_v, q_abs, q_pe, ckv, k_pe


def stick_breaking(q, segments, q_pos):
    z = jnp.concatenate([jnp.einsum('bqhd,bkhd->bhqk', q, k, preferred_element_type=F32)
                         for k, _, _ in segments], axis=-1) * SB_SCALE
    k_pos = jnp.concatenate([p for _, _, p in segments])
    visible = k_pos[None, :] < q_pos[:, None]
    log_1m_beta = jnp.where(visible, jax.nn.log_sigmoid(-z), 0.0)
    tail = lax.cumsum(log_1m_beta, axis=3, reverse=True)
    a = jnp.exp(jnp.where(visible, z + tail, -jnp.inf))
    out, off = 0.0, 0
    for _, v, _ in segments:
        n = v.shape[1]
        out = out + jnp.einsum('bhqk,bkhd->bqhd', a[..., off:off + n].astype(v.dtype), v)
        off += n
    return out


def mla_attend(q_abs, q_pe, segments, q_pos):
    s = jnp.concatenate([jnp.einsum('bqhc,bkc->bhqk', q_abs, ckv, preferred_element_type=F32)
                         + jnp.einsum('bqhr,bkr->bhqk', q_pe, kpe, preferred_element_type=F32)
                         for ckv, kpe, _ in segments], axis=-1) * MLA_SCALE
    k_pos = jnp.concatenate([p for _, _, p in segments])
    s = jnp.where(k_pos[None, :] <= q_pos[:, None], s, -jnp.inf)
    p = jax.nn.softmax(s, axis=-1)
    out, off = 0.0, 0
    for ckv, _, _ in segments:
        n = ckv.shape[1]
        out = out + jnp.einsum('bhqk,bkc->bqhc', p[..., off:off + n].astype(ckv.dtype), ckv)
        off += n
    return out


def mixer_output(o_sb, o_lat, w_uv, w_out):
    B, T = o_sb.shape[:2]
    o_mla = jnp.einsum('bthc,chv->bthv', o_lat, w_uv)
    o = jnp.concatenate([o_sb.reshape(B, T, SB_WIDTH), o_mla.reshape(B, T, MLA_WIDTH)], axis=-1)
    return o @ w_out


def swiglu(h, wg, wu, wd):
    return (jax.nn.silu(h @ wg) * (h @ wu)) @ wd


def moe_swiglu(h, w_router, b_router, wg, wu, wd):
    logits = jnp.einsum('btd,de->bte', h, w_router, preferred_element_type=F32) + b_router.astype(F32)
    top_val, top_idx = lax.top_k(logits, TOP_K)
    gate = jax.nn.softmax(top_val, axis=-1)
    combine = jnp.einsum('btk,btke->bte', gate, jax.nn.one_hot(top_idx, N_EXPERTS, dtype=F32))
    out = jnp.zeros_like(h)
    for e in range(N_EXPERTS):
        out = out + combine[..., e:e + 1].astype(h.dtype) * swiglu(h, wg[e], wu[e], wd[e])
    return out


def gather_pages(cache, layer, page_table):
    rows = cache[layer, page_table]
    return rows.reshape((page_table.shape[0], page_table.shape[1] * PAGE_SIZE) + cache.shape[3:])


def setup_inputs(seed: int = 0) -> dict:
    key = jax.random.key(seed)
    keys = iter(jax.random.split(key, 40))

    def nrm(shape, scale):
        return jax.random.normal(next(keys), shape, F32) * scale

    def gain(shape):
        return 1.0 + nrm(shape, 0.02)

    n_pages = PAST_LEN // PAGE_SIZE
    n_used = DEC_BATCH * n_pages
    n_pool = n_used + n_used // 4
    page_table = jax.random.permutation(next(keys), n_pool)[:n_used].reshape(DEC_BATCH, n_pages).astype(jnp.int32)
    d, f = D_MODEL, D_FF
    return {
        "x_prompt": nrm((BATCH, SEQ, d), 1.0),
        "x_sample": nrm((DEC_BATCH, DEC_SEQ, d), 1.0),
        "c_prompt": nrm((BATCH, d), 1.0),
        "c_sample": nrm((DEC_BATCH, d), 1.0),
        "cache_sb_k": jax.random.normal(next(keys), (DEPTH, n_pool, PAGE_SIZE, SB_HEADS, SB_HEAD_DIM), F32),
        "cache_sb_v": jax.random.normal(next(keys), (DEPTH, n_pool, PAGE_SIZE, SB_HEADS, SB_HEAD_DIM), F32),
        "cache_mla_ckv": jax.random.normal(next(keys), (DEPTH, n_pool, PAGE_SIZE, KV_RANK), F32),
        "cache_mla_kpe": jax.random.normal(next(keys), (DEPTH, n_pool, PAGE_SIZE, MLA_ROPE), F32),
        "page_table": page_table,
        "w_ada": nrm((DEPTH, d, 6 * d), 0.5 * d ** -0.5),
        "b_ada": nrm((DEPTH, 6 * d), 0.02),
        "g_mix": gain((DEPTH, d)),
        "g_ffn": gain((DEPTH, d)),
        "w_in": nrm((DEPTH, d, IN_COLS), d ** -0.5),
        "g_q": gain((DEPTH, Q_RANK)),
        "w_q_up": nrm((DEPTH, Q_RANK, MLA_HEADS * (MLA_NOPE + MLA_ROPE)), Q_RANK ** -0.5),
        "g_kv": gain((DEPTH, KV_RANK)),
        "w_uk": nrm((DEPTH, KV_RANK, MLA_HEADS, MLA_NOPE), KV_RANK ** -0.5),
        "w_uv": nrm((DEPTH, KV_RANK, MLA_HEADS, MLA_V), KV_RANK ** -0.5),
        "w_out": nrm((DEPTH, MIX_WIDTH, d), MIX_WIDTH ** -0.5),
        "w_gate_d": nrm((N_DENSE, d, f), d ** -0.5),
        "w_up_d": nrm((N_DENSE, d, f), d ** -0.5),
        "w_down_d": nrm((N_DENSE, f, d), f ** -0.5),
        "w_router": nrm((N_MOE, d, N_EXPERTS), d ** -0.5),
        "b_router": nrm((N_MOE, N_EXPERTS), 0.01),
        "w_gate_e": nrm((N_MOE, N_EXPERTS, d, f), d ** -0.5),
        "w_up_e": nrm((N_MOE, N_EXPERTS, d, f), d ** -0.5),
        "w_down_e": nrm((N_MOE, N_EXPERTS, f, d), f ** -0.5),
        "g_final": gain((d,)),
    }


def reference(x_prompt, x_sample, c_prompt, c_sample, cache_sb_k, cache_sb_v, cache_mla_ckv, cache_mla_kpe,
              page_table, w_ada, b_ada, g_mix, g_ffn, w_in, g_q, w_q_up, g_kv, w_uk, w_uv, w_out,
              w_gate_d, w_up_d, w_down_d, w_router, b_router, w_gate_e, w_up_e, w_down_e, g_final):
    t_prompt = x_prompt.shape[1]
    t_sample = x_sample.shape[1]
    past_len = page_table.shape[1] * PAGE_SIZE
    pos_p = jnp.arange(t_prompt)
    pos_s = past_len + jnp.arange(t_sample)
    past_pos = jnp.arange(past_len)

    xp, xs = x_prompt, x_sample
    new_p = ([], [], [], [])
    new_s = ([], [], [], [])
    for layer in range(DEPTH):
        mp = ada_mod(c_prompt, w_ada[layer], b_ada[layer])
        ms = ada_mod(c_sample, w_ada[layer], b_ada[layer])
        proj_w = (w_in[layer], g_q[layer], w_q_up[layer], g_kv[layer], w_uk[layer])

        hp = modulate(xp, g_mix[layer], mp[0], mp[1])
        sb_q, sb_k, sb_v, q_abs, q_pe, ckv, k_pe = mixer_inputs(hp, *proj_w, pos_p)
        o_sb, o_lat = [], []
        for start in range(0, t_prompt, Q_BLOCK):
            end = min(start + Q_BLOCK, t_prompt)
            qp = pos_p[start:end]
            o_sb.append(stick_breaking(sb_q[:, start:end], [(sb_k[:, :end], sb_v[:, :end], pos_p[:end])], qp))
            o_lat.append(mla_attend(q_abs[:, start:end], q_pe[:, start:end],
                                    [(ckv[:, :end], k_pe[:, :end], pos_p[:end])], qp))
        xp = xp + mp[2] * mixer_output(jnp.concatenate(o_sb, 1), jnp.concatenate(o_lat, 1), w_uv[layer], w_out[layer])
        for lst, arr in zip(new_p, (sb_k, sb_v, ckv, k_pe)):
            lst.append(arr)

        hs = modulate(xs, g_mix[layer], ms[0], ms[1])
        sb_q, sb_k, sb_v, q_abs, q_pe, ckv, k_pe = mixer_inputs(hs, *proj_w, pos_s)
        k_past = gather_pages(cache_sb_k, layer, page_table)
        v_past = gather_pages(cache_sb_v, layer, page_table)
        ckv_past = gather_pages(cache_mla_ckv, layer, page_table)
        kpe_past = gather_pages(cache_mla_kpe, layer, page_table)
        o_sb_s = stick_breaking(sb_q, [(k_past, v_past, past_pos), (sb_k, sb_v, pos_s)], pos_s)
        o_lat_s = mla_attend(q_abs, q_pe, [(ckv_past, kpe_past, past_pos), (ckv, k_pe, pos_s)], pos_s)
        xs = xs + ms[2] * mixer_output(o_sb_s, o_lat_s, w_uv[layer], w_out[layer])
        for lst, arr in zip(new_s, (sb_k, sb_v, ckv, k_pe)):
            lst.append(arr)

        hp = modulate(xp, g_ffn[layer], mp[3], mp[4])
        hs = modulate(xs, g_ffn[layer], ms[3], ms[4])
        i = layer // 2
        if layer % 2 == 0:
            fp = swiglu(hp, w_gate_d[i], w_up_d[i], w_down_d[i])
            fs = swiglu(hs, w_gate_d[i], w_up_d[i], w_down_d[i])
        else:
            fp = moe_swiglu(hp, w_router[i], b_router[i], w_gate_e[i], w_up_e[i], w_down_e[i])
            fs = moe_swiglu(hs, w_router[i], b_router[i], w_gate_e[i], w_up_e[i], w_down_e[i])
        xp = xp + mp[5] * fp
        xs = xs + ms[5] * fs

    y_prompt = rms_norm(xp, g_final)
    y_sample = rms_norm(xs, g_final)
    sb_k_prompt, sb_v_prompt, ckv_prompt, kpe_prompt = [jnp.stack(l, 0) for l in new_p]
    sb_k_sample, sb_v_sample, ckv_sample, kpe_sample = [jnp.stack(l, 0) for l in new_s]
    return (y_prompt, y_sample, sb_k_prompt, sb_v_prompt, ckv_prompt, kpe_prompt,
            sb_k_sample, sb_v_sample, ckv_sample, kpe_sample)
```

```python
import functools
import math

import jax
import jax.numpy as jnp
from jax import lax
from jax.experimental import pallas as pl
from jax.experimental.pallas import tpu as pltpu

F32 = jnp.float32
BF16 = jnp.bfloat16

D_MODEL = 1024
PAGE_SIZE = 128
SB_HEADS = 8
SB_HEAD_DIM = 64
SB_WIDTH = SB_HEADS * SB_HEAD_DIM
SB_SCALE = 1.0 / math.sqrt(SB_HEAD_DIM)
MLA_HEADS = 8
MLA_NOPE = 64
MLA_ROPE = 32
MLA_V = 64
Q_RANK = 256
KV_RANK = 128
MLA_SCALE = 1.0 / math.sqrt(MLA_NOPE + MLA_ROPE)
ROPE_THETA = 10000.0
D_FF = 2816
N_EXPERTS = 8
EPS = 1e-6

LANES = 128
QCAT = 2 * LANES
IN_EXT = 2048
NEG = -1e30

VMEM_LIMIT = 56 * 1024 * 1024


def _cp(sem):
    return pltpu.CompilerParams(dimension_semantics=sem, vmem_limit_bytes=VMEM_LIMIT)


def _rms(x, g):
    return x * lax.rsqrt(jnp.mean(x * x, axis=-1, keepdims=True) + EPS) * g


def _softplus(z):
    return jnp.maximum(z, 0.0) + jnp.log(1.0 + jnp.exp(-jnp.abs(z)))


def _split_bf16(x):
    hi = x.astype(BF16)
    lo = (x - hi.astype(F32)).astype(BF16)
    return hi, lo


def _dot_nt(a, b):
    return lax.dot_general(a, b, (((1,), (1,)), ((), ())), preferred_element_type=F32)


def _ada_kernel(c_ref, w_ref, b_ref, o_ref):
    c = c_ref[...]
    s = c * (1.0 / (1.0 + jnp.exp(-c)))
    o_ref[...] = jnp.dot(s.astype(BF16), w_ref[...].astype(BF16), preferred_element_type=F32) + b_ref[...]


def _ada(c_all, w_ada, b_ada):
    depth, d, n = w_ada.shape
    rows = c_all.shape[0]
    tn = 1536
    return pl.pallas_call(
        _ada_kernel,
        grid=(depth, n // tn),
        in_specs=[
            pl.BlockSpec((rows, d), lambda l, j: (0, 0)),
            pl.BlockSpec((None, d, tn), lambda l, j: (l, 0, j)),
            pl.BlockSpec((None, 1, tn), lambda l, j: (l, 0, j)),
        ],
        out_specs=pl.BlockSpec((None, rows, tn), lambda l, j: (l, 0, j)),
        out_shape=jax.ShapeDtypeStruct((depth, rows, n), F32),
        compiler_params=_cp(("parallel", "parallel")),
        name="ada_mod",
    )(c_all, w_ada, b_ada.reshape(depth, 1, n))


def _inproj_kernel(x_ref, shift_ref, scale_ref, g_ref, win_ref, gq_ref, wq_ref, gkv_ref, wuk_ref,
                   cos_ref, sin_ref,
                   sbq_ref, sbk_ref, sbv_ref, kb_ref, vb_ref, qcat_ref, ckv_ref, kpe_ref, kcat_ref):
    h = _rms(x_ref[...], g_ref[...]) * (1.0 + scale_ref[...]) + shift_ref[...]
    p = jnp.dot(h.astype(BF16), win_ref[...], preferred_element_type=F32)
    sbq_ref[...] = (p[:, :SB_WIDTH] * SB_SCALE).astype(BF16)
    k = p[:, SB_WIDTH:2 * SB_WIDTH]
    v = p[:, 2 * SB_WIDTH:3 * SB_WIDTH]
    sbk_ref[...] = k
    sbv_ref[...] = v
    kb_ref[...] = k.astype(BF16)
    vb_ref[...] = v.astype(BF16)

    c0 = 3 * SB_WIDTH
    qn = _rms(p[:, c0:c0 + Q_RANK], gq_ref[...])
    qq = jnp.dot(qn.astype(BF16), wq_ref[...], preferred_element_type=F32)
    qabs = jnp.dot(qq[:, :SB_WIDTH].astype(BF16), wuk_ref[...], preferred_element_type=F32)
    cos = cos_ref[...]
    sin = sin_ref[...]
    pe0 = MLA_HEADS * MLA_NOPE
    rh0 = pe0 + MLA_HEADS * LANES
    for hh in range(MLA_HEADS):
        pe = (qq[:, pe0 + LANES * hh:pe0 + LANES * (hh + 1)] * cos
              + qq[:, rh0 + LANES * hh:rh0 + LANES * (hh + 1)] * sin)
        qcat_ref[:, QCAT * hh:QCAT * hh + LANES] = qabs[:, LANES * hh:LANES * (hh + 1)].astype(BF16)
        qcat_ref[:, QCAT * hh + LANES:QCAT * (hh + 1)] = pe.astype(BF16)

    c1 = c0 + Q_RANK
    ckv = _rms(p[:, c1:c1 + KV_RANK], gkv_ref[...])
    ckv_ref[...] = ckv
    kg = p[:, c1 + KV_RANK:c1 + KV_RANK + LANES]
    kpe = kg * cos + pltpu.roll(kg, LANES - MLA_ROPE, 1) * sin
    kpe_ref[...] = kpe[:, :MLA_ROPE]
    kcat_ref[:, :KV_RANK] = ckv.astype(BF16)
    kcat_ref[:, KV_RANK:] = kpe.astype(BF16)


def _inproj(x, shift, scale, g, win, gq, wq, gkv, wuk, cos, sin, *, tm):
    rows, d = x.shape
    mod_rows = shift.shape[1]
    tiles_per_batch = cos.shape[0] // tm
    mod_map = lambda i: (i // tiles_per_batch, 0, 0)
    pos_map = lambda i: (i % tiles_per_batch, 0)
    row_map = lambda i: (i, 0)
    full = lambda i: (0, 0)
    wide = [(SB_WIDTH, BF16), (SB_WIDTH, F32), (SB_WIDTH, F32), (SB_WIDTH, BF16), (SB_WIDTH, BF16),
            (MLA_HEADS * QCAT, BF16), (KV_RANK, F32), (MLA_ROPE, F32), (QCAT, BF16)]
    return pl.pallas_call(
        _inproj_kernel,
        grid=(rows // tm,),
        in_specs=[
            pl.BlockSpec((tm, d), row_map),
            pl.BlockSpec((None, mod_rows, d), mod_map),
            pl.BlockSpec((None, mod_rows, d), mod_map),
            pl.BlockSpec((1, d), full),
            pl.BlockSpec(win.shape, full),
            pl.BlockSpec((1, Q_RANK), full),
            pl.BlockSpec(wq.shape, full),
            pl.BlockSpec((1, KV_RANK), full),
            pl.BlockSpec(wuk.shape, full),
            pl.BlockSpec((tm, LANES), pos_map),
            pl.BlockSpec((tm, LANES), pos_map),
        ],
        out_specs=[pl.BlockSpec((tm, w), row_map) for w, _ in wide],
        out_shape=[jax.ShapeDtypeStruct((rows, w), dt) for w, dt in wide],
        compiler_params=_cp(("parallel",)),
        name="inproj",
    )(x, shift, scale, g, win, gq, wq, gkv, wuk, cos, sin)


def _sb_kernel(q_ref, k_ref, v_ref, u_ref, o_ref, *, tq, tk):
    qi = pl.program_id(2)
    q = q_ref[...]
    lane = lax.broadcasted_iota(jnp.int32, (tq, LANES), 1)
    qpos = qi * tq + lax.broadcasted_iota(jnp.int32, (tq, tk), 0)
    kcol = lax.broadcasted_iota(jnp.int32, (tq, tk), 1)
    u = u_ref[...]
    nkb = (qi + 1) * (tq // tk)
    outs = []
    for hh in range(2):
        in_head = (lane >= SB_HEAD_DIM * hh) & (lane < SB_HEAD_DIM * (hh + 1))
        qm = jnp.where(in_head, q.astype(F32), 0.0).astype(BF16)

        def body(i, carry):
            acc, run = carry
            kb = nkb - 1 - i
            start = pl.multiple_of(kb * tk, tk)
            kblk = k_ref[pl.ds(start, tk), :]
            vblk = v_ref[pl.ds(start, tk), :]
            z = _dot_nt(qm, kblk)
            vis = (kb * tk + kcol) < qpos
            l1m = jnp.where(vis, -_softplus(z), 0.0)
            hi, lo = _split_bf16(l1m)
            tail = (jnp.dot(hi, u, preferred_element_type=F32)
                    + jnp.dot(lo, u, preferred_element_type=F32)) + run
            a = jnp.where(vis, jnp.exp(z + tail), 0.0)
            acc = acc + jnp.dot(a.astype(BF16), vblk, preferred_element_type=F32)
            run = run + jnp.sum(l1m, axis=1, keepdims=True)
            return acc, run

        acc, _ = lax.fori_loop(0, nkb, body, (jnp.zeros((tq, LANES), F32), jnp.zeros((tq, 1), F32)))
        outs.append(acc)
    o_ref[...] = jnp.where(lane < SB_HEAD_DIM, outs[0], outs[1]).astype(o_ref.dtype)


def _sb_attention(q, k, v, u, *, tq, tk):
    b, t, w = q.shape
    kern = functools.partial(_sb_kernel, tq=tq, tk=tk)
    return pl.pallas_call(
        kern,
        grid=(b, w // LANES, t // tq),
        in_specs=[
            pl.BlockSpec((None, tq, LANES), lambda bi, hp, qi: (bi, qi, hp)),
            pl.BlockSpec((None, t, LANES), lambda bi, hp, qi: (bi, 0, hp)),
            pl.BlockSpec((None, t, LANES), lambda bi, hp, qi: (bi, 0, hp)),
            pl.BlockSpec((tk, tk), lambda bi, hp, qi: (0, 0)),
        ],
        out_specs=pl.BlockSpec((None, tq, LANES), lambda bi, hp, qi: (bi, qi, hp)),
        out_shape=jax.ShapeDtypeStruct((b, t, w), BF16),
        compiler_params=_cp(("parallel", "parallel", "parallel")),
        name="sb_attn",
    )(q, k, v, u)


def _mla_kernel(q_ref, k_ref, o_ref, *, tq, tk):
    qi = pl.program_id(1)
    rows = MLA_HEADS * tq
    qs = jnp.concatenate([q_ref[:, QCAT * hh:QCAT * (hh + 1)] for hh in range(MLA_HEADS)], axis=0)
    qpos = qi * tq + (lax.broadcasted_iota(jnp.int32, (rows, tk), 0) & (tq - 1))
    kcol = lax.broadcasted_iota(jnp.int32, (rows, tk), 1)
    nkb = (qi * tq + tq + tk - 1) // tk

    def body(j, carry):
        m, l, acc = carry
        start = pl.multiple_of(j * tk, tk)
        kblk = k_ref[pl.ds(start, tk), :]
        s = _dot_nt(qs, kblk) * MLA_SCALE
        s = jnp.where((j * tk + kcol) <= qpos, s, NEG)
        m_new = jnp.maximum(m, jnp.max(s, axis=1, keepdims=True))
        alpha = jnp.exp(m - m_new)
        p = jnp.exp(s - m_new)
        l = alpha * l + jnp.sum(p, axis=1, keepdims=True)
        acc = alpha * acc + jnp.dot(p.astype(BF16), kblk[:, :KV_RANK], preferred_element_type=F32)
        return m_new, l, acc

    init = (jnp.full((rows, 1), NEG, F32), jnp.zeros((rows, 1), F32), jnp.zeros((rows, KV_RANK), F32))
    _, l, acc = lax.fori_loop(0, nkb, body, init)
    out = acc / l
    for hh in range(MLA_HEADS):
        o_ref[:, KV_RANK * hh:KV_RANK * (hh + 1)] = out[tq * hh:tq * (hh + 1)].astype(o_ref.dtype)


def _mla_attention(qcat, kcat, *, tq, tk):
    b, t, _ = qcat.shape
    kern = functools.partial(_mla_kernel, tq=tq, tk=tk)
    return pl.pallas_call(
        kern,
        grid=(b, t // tq),
        in_specs=[
            pl.BlockSpec((None, tq, MLA_HEADS * QCAT), lambda bi, qi: (bi, qi, 0)),
            pl.BlockSpec((None, t, QCAT), lambda bi, qi: (bi, 0, 0)),
        ],
        out_specs=pl.BlockSpec((None, tq, MLA_HEADS * KV_RANK), lambda bi, qi: (bi, qi, 0)),
        out_shape=jax.ShapeDtypeStruct((b, t, MLA_HEADS * KV_RANK), BF16),
        compiler_params=_cp(("parallel", "parallel")),
        name="mla_attn",
    )(qcat, kcat)


PAGES = 8
SEG = 256
QROWS = 16


def _sample_attn_kernel(pt_ref, qbd_ref, qcat_ref, knew_ref, vnew_ref, kcnew_ref, u_ref, *rest, past_len):
    pages = rest[:4 * PAGES]
    osb_ref, olat_ref = rest[4 * PAGES:4 * PAGES + 2]
    acc_sb, run_ref, m_ref, l_ref, acc_lat = rest[4 * PAGES + 2:]
    kp, vp, cp, pp = (pages[i * PAGES:(i + 1) * PAGES] for i in range(4))
    c = pl.program_id(1)
    qbd = qbd_ref[...]
    qcat = qcat_ref[...]

    @pl.when(c == 0)
    def _():
        kpos = jnp.full((QROWS, 1), past_len, jnp.int32)
        qpos = jnp.full((QROWS, 1), past_len, jnp.int32)
        z = jnp.sum(qbd.astype(F32) * knew_ref[...].astype(F32), axis=1, keepdims=True)
        vis = kpos < qpos
        l1m = jnp.where(vis, -_softplus(z), 0.0)
        a = jnp.where(vis, jnp.exp(z + l1m), 0.0)
        acc_sb[...] = a * vnew_ref[...].astype(F32)
        run_ref[...] = jnp.broadcast_to(l1m, (QROWS, LANES))
        kc = kcnew_ref[...].astype(F32)
        s = jnp.sum(qcat.astype(F32) * kc, axis=1, keepdims=True) * MLA_SCALE
        vis2 = kpos <= qpos
        m_ref[...] = jnp.broadcast_to(jnp.where(vis2, s, NEG), (QROWS, LANES))
        p = jnp.where(vis2, 1.0, 0.0)
        l_ref[...] = jnp.broadcast_to(p, (QROWS, LANES))
        acc_lat[...] = p * kc[:, :KV_RANK]

    nseg = PAGES * PAGE_SIZE // SEG
    kc = jnp.concatenate([r[...] for r in kp], axis=1).astype(BF16)
    z = jnp.dot(qbd, kc, preferred_element_type=F32)
    l1m = -_softplus(z)
    l4 = jnp.concatenate([l1m[:, SEG * s:SEG * (s + 1)] for s in range(nseg)], axis=0)
    hi, lo = _split_bf16(l4)
    u = u_ref[...]
    cum = jnp.dot(hi, u, preferred_element_type=F32) + jnp.dot(lo, u, preferred_element_type=F32)
    tot = jnp.sum(l4, axis=1, keepdims=True)
    carry = run_ref[:, :1]
    tails = [None] * nseg
    for s in reversed(range(nseg)):
        tails[s] = cum[QROWS * s:QROWS * (s + 1)] + carry
        carry = carry + tot[QROWS * s:QROWS * (s + 1)]
    run_ref[...] = jnp.broadcast_to(carry, (QROWS, LANES))
    a = jnp.exp(z + jnp.concatenate(tails, axis=1))
    vc = jnp.concatenate([r[...] for r in vp], axis=1).astype(BF16)
    acc_sb[...] += _dot_nt(a.astype(BF16), vc)

    ck = jnp.concatenate([r[...] for r in cp], axis=0).astype(BF16)
    pk = jnp.concatenate([r[...] for r in pp], axis=1).astype(BF16)
    s = (_dot_nt(qcat[:, :KV_RANK], ck)
         + jnp.dot(qcat[:, KV_RANK:KV_RANK + MLA_ROPE], pk, preferred_element_type=F32)) * MLA_SCALE
    m_old = m_ref[:, :1]
    m_new = jnp.maximum(m_old, jnp.max(s, axis=1, keepdims=True))
    alpha = jnp.exp(m_old - m_new)
    p = jnp.exp(s - m_new)
    l_new = alpha * l_ref[:, :1] + jnp.sum(p, axis=1, keepdims=True)
    acc_lat[...] = alpha * acc_lat[...] + jnp.dot(p.astype(BF16), ck, preferred_element_type=F32)
    m_ref[...] = jnp.broadcast_to(m_new, (QROWS, LANES))
    l_ref[...] = jnp.broadcast_to(l_new, (QROWS, LANES))

    @pl.when(c == pl.num_programs(1) - 1)
    def _():
        acc = acc_sb[...]
        row = lax.broadcasted_iota(jnp.int32, acc.shape, 0)
        head = lax.broadcasted_iota(jnp.int32, acc.shape, 1) // SB_HEAD_DIM
        osb_ref[...] = jnp.sum(jnp.where(row == head, acc, 0.0), axis=0, keepdims=True).astype(osb_ref.dtype)
        olat_ref[...] = (acc_lat[...] / l_ref[...])[:MLA_HEADS].astype(olat_ref.dtype)


def _sample_attention(layer, page_table, qbd, qcat, knew, vnew, kcnew, u, cache_k, cache_v, cache_c, cache_p):
    nb, n_pages = page_table.shape
    nchunk = n_pages // PAGES
    past_len = n_pages * PAGE_SIZE

    def page_spec(shape, i):
        def imap(b, c, pt):
            return (layer, pt[b, (nchunk - 1 - c) * PAGES + i], 0, 0)
        return pl.BlockSpec((None, None) + shape, imap)

    per_b = lambda b, c, pt: (b, 0, 0)
    in_specs = [
        pl.BlockSpec((None, QROWS, SB_WIDTH), per_b),
        pl.BlockSpec((None, QROWS, QCAT), per_b),
        pl.BlockSpec((None, 1, SB_WIDTH), per_b),
        pl.BlockSpec((None, 1, SB_WIDTH), per_b),
        pl.BlockSpec((None, 1, QCAT), per_b),
        pl.BlockSpec((SEG, SEG), lambda b, c, pt: (0, 0)),
    ]
    operands = [qbd, qcat, knew, vnew, kcnew, u]
    for cache in (cache_k, cache_v, cache_c, cache_p):
        for i in range(PAGES):
            in_specs.append(page_spec(cache.shape[2:], i))
            operands.append(cache)
    grid_spec = pltpu.PrefetchScalarGridSpec(
        num_scalar_prefetch=1,
        grid=(nb, nchunk),
        in_specs=in_specs,
        out_specs=[
            pl.BlockSpec((None, 1, SB_WIDTH), per_b),
            pl.BlockSpec((None, MLA_HEADS, KV_RANK), per_b),
        ],
        scratch_shapes=[
            pltpu.VMEM((QROWS, SB_WIDTH), F32),
            pltpu.VMEM((QROWS, LANES), F32),
            pltpu.VMEM((QROWS, LANES), F32),
            pltpu.VMEM((QROWS, LANES), F32),
            pltpu.VMEM((QROWS, KV_RANK), F32),
        ],
    )
    return pl.pallas_call(
        functools.partial(_sample_attn_kernel, past_len=past_len),
        grid_spec=grid_spec,
        out_shape=[
            jax.ShapeDtypeStruct((nb, 1, SB_WIDTH), BF16),
            jax.ShapeDtypeStruct((nb, MLA_HEADS, KV_RANK), BF16),
        ],
        compiler_params=_cp(("parallel", "arbitrary")),
        name="sample_attn",
    )(page_table, *operands)


def _outproj_kernel(*refs, moe):
    (osb_ref, olat_ref, x_ref, gate_ref, wuv_ref, wout_ref, g_ref, shift_ref, scale_ref) = refs[:9]
    if moe:
        wr_ref, br_ref, xo_ref, h_ref, comb_ref = refs[9:]
    else:
        xo_ref, h_ref = refs[9:]
    omla = jnp.dot(olat_ref[...], wuv_ref[...], preferred_element_type=F32)
    o = jnp.concatenate([osb_ref[...], omla.astype(BF16)], axis=1)
    x = x_ref[...] + gate_ref[...] * jnp.dot(o, wout_ref[...], preferred_element_type=F32)
    xo_ref[...] = x
    h = _rms(x, g_ref[...]) * (1.0 + scale_ref[...]) + shift_ref[...]
    h_ref[...] = h.astype(BF16)
    if moe:
        logits = jnp.dot(h.astype(BF16), wr_ref[...].astype(BF16), preferred_element_type=F32) + br_ref[...]
        lane = lax.broadcasted_iota(jnp.int32, logits.shape, 1).astype(F32)
        lg = jnp.where(lane < N_EXPERTS, logits, -jnp.inf)
        m1 = jnp.max(lg, axis=1, keepdims=True)
        i1 = jnp.min(jnp.where(lg == m1, lane, float(LANES)), axis=1, keepdims=True)
        lg2 = jnp.where(lane == i1, -jnp.inf, lg)
        m2 = jnp.max(lg2, axis=1, keepdims=True)
        i2 = jnp.min(jnp.where(lg2 == m2, lane, float(LANES)), axis=1, keepdims=True)
        e = jnp.exp(m2 - m1)
        g1 = 1.0 / (1.0 + e)
        g2 = e / (1.0 + e)
        comb_ref[...] = jnp.where(lane == i1, g1, 0.0) + jnp.where(lane == i2, g2, 0.0)


def _outproj(osb, olat, x, gate, wuv, wout, g, shift, scale, router, *, tm, tiles_per_batch):
    rows, d = x.shape
    mod_rows = gate.shape[1]
    moe = router is not None
    mod_map = lambda i: (i // tiles_per_batch, 0, 0)
    row_map = lambda i: (i, 0)
    full = lambda i: (0, 0)
    in_specs = [
        pl.BlockSpec((tm, SB_WIDTH), row_map),
        pl.BlockSpec((tm, MLA_HEADS * KV_RANK), row_map),
        pl.BlockSpec((tm, d), row_map),
        pl.BlockSpec((None, mod_rows, d), mod_map),
        pl.BlockSpec(wuv.shape, full),
        pl.BlockSpec(wout.shape, full),
        pl.BlockSpec((1, d), full),
        pl.BlockSpec((None, mod_rows, d), mod_map),
        pl.BlockSpec((None, mod_rows, d), mod_map),
    ]
    operands = [osb, olat, x, gate, wuv, wout, g, shift, scale]
    out_specs = [pl.BlockSpec((tm, d), row_map), pl.BlockSpec((tm, d), row_map)]
    out_shape = [jax.ShapeDtypeStruct((rows, d), F32), jax.ShapeDtypeStruct((rows, d), BF16)]
    if moe:
        in_specs += [pl.BlockSpec((d, LANES), full), pl.BlockSpec((1, LANES), full)]
        operands += list(router)
        out_specs.append(pl.BlockSpec((tm, LANES), row_map))
        out_shape.append(jax.ShapeDtypeStruct((rows, LANES), F32))
    return pl.pallas_call(
        functools.partial(_outproj_kernel, moe=moe),
        grid=(rows // tm,),
        in_specs=in_specs,
        out_specs=out_specs,
        out_shape=out_shape,
        compiler_params=_cp(("parallel",)),
        name="outproj",
    )(*operands)


def _ffn_kernel(*refs, moe):
    if moe:
        h_ref, wg_ref, wu_ref, wd_ref, x_ref, gate_ref, comb_ref, o_ref, acc_ref = refs
    else:
        h_ref, wg_ref, wu_ref, wd_ref, x_ref, gate_ref, o_ref, acc_ref = refs
    e = pl.program_id(1)
    j = pl.program_id(2)

    @pl.when((e == 0) & (j == 0))
    def _():
        acc_ref[...] = jnp.zeros_like(acc_ref)

    h = h_ref[...]
    gt = jnp.dot(h, wg_ref[...], preferred_element_type=F32)
    up = jnp.dot(h, wu_ref[...], preferred_element_type=F32)
    act = gt * (1.0 / (1.0 + jnp.exp(-gt))) * up
    y = jnp.dot(act.astype(BF16), wd_ref[...], preferred_element_type=F32)
    if moe:
        comb = comb_ref[...]
        lane = lax.broadcasted_iota(jnp.int32, comb.shape, 1)
        y = jnp.sum(jnp.where(lane == e, comb, 0.0), axis=1, keepdims=True) * y
    acc_ref[...] += y

    @pl.when((e == pl.num_programs(1) - 1) & (j == pl.num_programs(2) - 1))
    def _():
        o_ref[...] = x_ref[...] + gate_ref[...] * acc_ref[...]


def _ffn(h, wg, wu, wd, x, gate, comb, *, tm, tf, tiles_per_batch):
    rows, d = x.shape
    ne, _, f = wg.shape
    mod_rows = gate.shape[1]
    moe = comb is not None
    row_map = lambda i, e, j: (i, 0)
    in_specs = [
        pl.BlockSpec((tm, d), row_map),
        pl.BlockSpec((None, d, tf), lambda i, e, j: (e, 0, j)),
        pl.BlockSpec((None, d, tf), lambda i, e, j: (e, 0, j)),
        pl.BlockSpec((None, tf, d), lambda i, e, j: (e, j, 0)),
        pl.BlockSpec((tm, d), row_map),
        pl.BlockSpec((None, mod_rows, d), lambda i, e, j: (i // tiles_per_batch, 0, 0)),
    ]
    operands = [h, wg, wu, wd, x, gate]
    if moe:
        in_specs.append(pl.BlockSpec((tm, LANES), row_map))
        operands.append(comb)
    return pl.pallas_call(
        functools.partial(_ffn_kernel, moe=moe),
        grid=(rows // tm, ne, f // tf),
        in_specs=in_specs,
        out_specs=pl.BlockSpec((tm, d), row_map),
        out_shape=jax.ShapeDtypeStruct((rows, d), F32),
        scratch_shapes=[pltpu.VMEM((tm, d), F32)],
        compiler_params=_cp(("parallel", "arbitrary", "arbitrary")),
        name="ffn",
    )(*operands)


def _norm_kernel(x_ref, g_ref, o_ref):
    o_ref[...] = _rms(x_ref[...], g_ref[...])


def _final_norm(x, g, *, tm):
    rows, d = x.shape
    return pl.pallas_call(
        _norm_kernel,
        grid=(rows // tm,),
        in_specs=[pl.BlockSpec((tm, d), lambda i: (i, 0)), pl.BlockSpec((1, d), lambda i: (0, 0))],
        out_specs=pl.BlockSpec((tm, d), lambda i: (i, 0)),
        out_shape=jax.ShapeDtypeStruct((rows, d), F32),
        compiler_params=_cp(("parallel",)),
        name="final_norm",
    )(x, g)


def _rotate_half_cols(w):
    half = MLA_ROPE // 2
    return jnp.concatenate([-w[..., half:], w[..., :half]], axis=-1)


def _prep_layer_weights(w_in, w_q_up, w_uk, w_uv, w_out):
    d = w_in.shape[0]
    c_kpe = 3 * SB_WIDTH + Q_RANK + KV_RANK
    kpe_w = w_in[:, c_kpe:c_kpe + MLA_ROPE]
    win = jnp.concatenate(
        [w_in, _rotate_half_cols(kpe_w), jnp.zeros((d, IN_EXT - c_kpe - 2 * MLA_ROPE), F32)], axis=1).astype(BF16)

    wq = w_q_up.reshape(Q_RANK, MLA_HEADS, MLA_NOPE + MLA_ROPE)
    nope = wq[:, :, :MLA_NOPE].reshape(Q_RANK, MLA_HEADS * MLA_NOPE)
    pe = wq[:, :, MLA_NOPE:]
    pad = jnp.zeros((Q_RANK, MLA_HEADS, LANES - MLA_ROPE), F32)
    pe_ext = jnp.concatenate([pe, pad], axis=2).reshape(Q_RANK, MLA_HEADS * LANES)
    rh_ext = jnp.concatenate([_rotate_half_cols(pe), pad], axis=2).reshape(Q_RANK, MLA_HEADS * LANES)
    wq_all = jnp.concatenate([nope, pe_ext, rh_ext], axis=1).astype(BF16)

    eye = jnp.eye(MLA_HEADS, dtype=F32)
    wuk_bd = jnp.einsum('chn,hg->hngc', w_uk, eye).reshape(MLA_HEADS * MLA_NOPE, MLA_HEADS * KV_RANK).astype(BF16)
    wuv_bd = jnp.einsum('chv,hg->hcgv', w_uv, eye).reshape(MLA_HEADS * KV_RANK, MLA_HEADS * MLA_V).astype(BF16)
    return win, wq_all, wuk_bd, wuv_bd, w_out.astype(BF16)


def _rope_tables(pos):
    inv = ROPE_THETA ** (-jnp.arange(0, MLA_ROPE, 2, dtype=F32) / MLA_ROPE)
    ang = pos.astype(F32)[:, None] * inv[None, :]
    pad = jnp.zeros((pos.shape[0], LANES - MLA_ROPE), F32)
    cos = jnp.concatenate([jnp.cos(ang), jnp.cos(ang), pad], axis=1)
    sin = jnp.concatenate([jnp.sin(ang), jnp.sin(ang), pad], axis=1)
    return cos, sin


def _row_tile(rows, target):
    return min(rows, target)


def kernel(x_prompt, x_sample, c_prompt, c_sample, cache_sb_k, cache_sb_v, cache_mla_ckv, cache_mla_kpe, page_table, w_ada, b_ada, g_mix, g_ffn, w_in, g_q, w_q_up, g_kv, w_uk, w_uv, w_out, w_gate_d, w_up_d, w_down_d, w_router, b_router, w_gate_e, w_up_e, w_down_e, g_final):
    nbp, t, d = x_prompt.shape
    nbs = x_sample.shape[0]
    depth = w_in.shape[0]
    n_pool = cache_sb_k.shape[1]
    past_len = page_table.shape[1] * PAGE_SIZE
    assert x_sample.shape[1] == 1 and page_table.shape[1] % PAGES == 0

    xp = x_prompt.reshape(nbp * t, d)
    xs = x_sample.reshape(nbs, d)
    tm_p = _row_tile(t, 512)
    tpb_p = t // tm_p
    tq_sb = _row_tile(t, 256)
    tq_mla = _row_tile(t, 128)

    mod = _ada(jnp.concatenate([c_prompt, c_sample], axis=0), w_ada, b_ada)
    mod = mod.reshape(depth, nbp + nbs, 6, d)
    mod_p = mod[:, :nbp].transpose(0, 2, 1, 3).reshape(depth, 6, nbp, 1, d)
    mod_s = mod[:, nbp:].transpose(0, 2, 1, 3).reshape(depth, 6, 1, nbs, d)

    cos_p, sin_p = _rope_tables(jnp.arange(t))
    cos_s, sin_s = _rope_tables(jnp.full((nbs,), past_len))
    u_sb = jnp.tril(jnp.ones((tq_sb, tq_sb), F32)).astype(BF16)
    u_seg = jnp.tril(jnp.ones((SEG, SEG), F32)).astype(BF16)

    ck = jnp.transpose(cache_sb_k, (0, 1, 3, 4, 2)).reshape(depth, n_pool, SB_WIDTH, PAGE_SIZE)
    cv = jnp.transpose(cache_sb_v, (0, 1, 3, 4, 2)).reshape(depth, n_pool, SB_WIDTH, PAGE_SIZE)
    cpe = jnp.transpose(cache_mla_kpe, (0, 1, 3, 2))
    head_of_lane = jnp.arange(SB_WIDTH) // SB_HEAD_DIM
    head_mask = (jnp.arange(QROWS)[:, None] == head_of_lane[None, :])

    new_p = ([], [], [], [])
    new_s = ([], [], [], [])
    for layer in range(depth):
        win, wq_all, wuk_bd, wuv_bd, wout = _prep_layer_weights(
            w_in[layer], w_q_up[layer], w_uk[layer], w_uv[layer], w_out[layer])
        gm = g_mix[layer].reshape(1, d)
        gf = g_ffn[layer].reshape(1, d)
        gq = g_q[layer].reshape(1, Q_RANK)
        gkv = g_kv[layer].reshape(1, KV_RANK)
        i = layer // 2
        if layer % 2 == 0:
            router = None
            wg, wu, wd = (w[i][None].astype(BF16) for w in (w_gate_d, w_up_d, w_down_d))
        else:
            wr = jnp.concatenate([w_router[i], jnp.zeros((d, LANES - N_EXPERTS), F32)], axis=1)
            br = jnp.concatenate([b_router[i], jnp.zeros((LANES - N_EXPERTS,), F32)]).reshape(1, LANES)
            router = (wr, br)
            wg, wu, wd = (w[i].astype(BF16) for w in (w_gate_e, w_up_e, w_down_e))

        mp = mod_p[layer]
        sbq, sbk, sbv, kb, vb, qcat, ckv, kpe, kcat = _inproj(
            xp, mp[0], mp[1], gm, win, gq, wq_all, gkv, wuk_bd, cos_p, sin_p, tm=tm_p)
        o_sb = _sb_attention(sbq.reshape(nbp, t, SB_WIDTH), kb.reshape(nbp, t, SB_WIDTH),
                             vb.reshape(nbp, t, SB_WIDTH), u_sb, tq=tq_sb, tk=tq_sb)
        o_lat = _mla_attention(qcat.reshape(nbp, t, MLA_HEADS * QCAT), kcat.reshape(nbp, t, QCAT),
                               tq=tq_mla, tk=min(t, 256))
        res = _outproj(o_sb.reshape(nbp * t, SB_WIDTH), o_lat.reshape(nbp * t, MLA_HEADS * KV_RANK), xp, mp[2],
                       wuv_bd, wout, gf, mp[3], mp[4], router, tm=tm_p, tiles_per_batch=tpb_p)
        xp, hp = res[0], res[1]
        xp = _ffn(hp, wg, wu, wd, xp, mp[5], res[2] if router else None, tm=tm_p, tf=D_FF // 2,
                  tiles_per_batch=tpb_p)
        for lst, arr in zip(new_p, (sbk, sbv, ckv, kpe)):
            lst.append(arr)

        ms = mod_s[layer]
        sbq, sbk, sbv, kb, vb, qcat, ckv, kpe, kcat = _inproj(
            xs, ms[0], ms[1], gm, win, gq, wq_all, gkv, wuk_bd, cos_s, sin_s, tm=nbs)
        qbd = jnp.where(head_mask[None], sbq[:, None, :], jnp.zeros((), BF16))
        qc = qcat.reshape(nbs, MLA_HEADS, QCAT)
        qc = jnp.concatenate([qc, jnp.zeros((nbs, QROWS - MLA_HEADS, QCAT), BF16)], axis=1)
        o_sb, o_lat = _sample_attention(
            layer, page_table, qbd, qc, kb.reshape(nbs, 1, SB_WIDTH), vb.reshape(nbs, 1, SB_WIDTH),
            kcat.reshape(nbs, 1, QCAT), u_seg, ck, cv, cache_mla_ckv, cpe)
        res = _outproj(o_sb.reshape(nbs, SB_WIDTH), o_lat.reshape(nbs, MLA_HEADS * KV_RANK), xs, ms[2],
                       wuv_bd, wout, gf, ms[3], ms[4], router, tm=nbs, tiles_per_batch=1)
        xs, hs = res[0], res[1]
        xs = _ffn(hs, wg, wu, wd, xs, ms[5], res[2] if router else None, tm=nbs, tf=D_FF // 2, tiles_per_batch=1)
        for lst, arr in zip(new_s, (sbk, sbv, ckv, kpe)):
            lst.append(arr)

    gfin = g_final.reshape(1, d)
    y_prompt = _final_norm(xp, gfin, tm=tm_p).reshape(nbp, t, d)
    y_sample = _final_norm(xs, gfin, tm=nbs).reshape(nbs, 1, d)

    def stack(lst, nb, tt, tail):
        return jnp.stack(lst, 0).reshape((depth, nb, tt) + tail)

    hd = (SB_HEADS, SB_HEAD_DIM)
    return (y_prompt, y_sample,
            stack(new_p[0], nbp, t, hd), stack(new_p[1], nbp, t, hd),
            stack(new_p[2], nbp, t, (KV_RANK,)), stack(new_p[3], nbp, t, (MLA_ROPE,)),
            stack(new_s[0], nbs, 1, hd), stack(new_s[1], nbs, 1, hd),
            stack(new_s[2], nbs, 1, (KV_RANK,)), stack(new_s[3], nbs, 1, (MLA_ROPE,)))
```

```python
import functools
import math

import jax
import jax.numpy as jnp
from jax import lax
from jax.experimental import pallas as pl
from jax.experimental.pallas import tpu as pltpu

F32 = jnp.float32
BF16 = jnp.bfloat16

D_MODEL = 1024
PAGE_SIZE = 128
SB_HEADS = 8
SB_HEAD_DIM = 64
SB_WIDTH = SB_HEADS * SB_HEAD_DIM
SB_SCALE = 1.0 / math.sqrt(SB_HEAD_DIM)
MLA_HEADS = 8
MLA_NOPE = 64
MLA_ROPE = 32
MLA_V = 64
Q_RANK = 256
KV_RANK = 128
MLA_SCALE = 1.0 / math.sqrt(MLA_NOPE + MLA_ROPE)
ROPE_THETA = 10000.0
D_FF = 2816
N_EXPERTS = 8
EPS = 1e-6

LANES = 128
QCAT = 2 * LANES
IN_EXT = 2048
NEG = -1e30
SB_SUB = 256

VMEM_LIMIT = 56 * 1024 * 1024


def _cp(sem):
    return pltpu.CompilerParams(dimension_semantics=sem, vmem_limit_bytes=VMEM_LIMIT)


def _rms(x, g):
    return x * lax.rsqrt(jnp.mean(x * x, axis=-1, keepdims=True) + EPS) * g


def _softplus(z):
    return jnp.maximum(z, 0.0) + jnp.log(1.0 + jnp.exp(-jnp.abs(z)))


def _split_bf16(x):
    hi = x.astype(BF16)
    lo = (x - hi.astype(F32)).astype(BF16)
    return hi, lo


def _dot_nt(a, b):
    return lax.dot_general(a, b, (((1,), (1,)), ((), ())), preferred_element_type=F32)


def _ada_kernel(c_ref, w_ref, b_ref, o_ref):
    c = c_ref[...]
    s = c * (1.0 / (1.0 + jnp.exp(-c)))
    o_ref[...] = jnp.dot(s.astype(BF16), w_ref[...].astype(BF16), preferred_element_type=F32) + b_ref[...]


def _ada(c_all, w_ada, b_ada):
    depth, d, n = w_ada.shape
    rows = c_all.shape[0]
    tn = 1536
    return pl.pallas_call(
        _ada_kernel,
        grid=(depth, n // tn),
        in_specs=[
            pl.BlockSpec((rows, d), lambda l, j: (0, 0)),
            pl.BlockSpec((None, d, tn), lambda l, j: (l, 0, j)),
            pl.BlockSpec((None, 1, tn), lambda l, j: (l, 0, j)),
        ],
        out_specs=pl.BlockSpec((None, rows, tn), lambda l, j: (l, 0, j)),
        out_shape=jax.ShapeDtypeStruct((depth, rows, n), F32),
        compiler_params=_cp(("parallel", "parallel")),
        name="ada_mod",
    )(c_all, w_ada, b_ada.reshape(depth, 1, n))


def _inproj_kernel(x_ref, shift_ref, scale_ref, g_ref, win_ref, gq_ref, wq_ref, gkv_ref, wuk_ref,
                   cos_ref, sin_ref,
                   sbq_ref, sbk_ref, sbv_ref, kb_ref, vb_ref, qcat_ref, ckv_ref, kpe_ref, kcat_ref):
    h = _rms(x_ref[...], g_ref[...]) * (1.0 + scale_ref[...]) + shift_ref[...]
    p = jnp.dot(h.astype(BF16), win_ref[...], preferred_element_type=F32)
    sbq_ref[...] = (p[:, :SB_WIDTH] * SB_SCALE).astype(BF16)
    k = p[:, SB_WIDTH:2 * SB_WIDTH]
    v = p[:, 2 * SB_WIDTH:3 * SB_WIDTH]
    sbk_ref[...] = k
    sbv_ref[...] = v
    kb_ref[...] = k.astype(BF16)
    vb_ref[...] = v.astype(BF16)

    c0 = 3 * SB_WIDTH
    qn = _rms(p[:, c0:c0 + Q_RANK], gq_ref[...])
    qq = jnp.dot(qn.astype(BF16), wq_ref[...], preferred_element_type=F32)
    qabs = jnp.dot(qq[:, :SB_WIDTH].astype(BF16), wuk_ref[...], preferred_element_type=F32)
    cos = cos_ref[...]
    sin = sin_ref[...]
    pe0 = MLA_HEADS * MLA_NOPE
    rh0 = pe0 + MLA_HEADS * LANES
    for hh in range(MLA_HEADS):
        pe = (qq[:, pe0 + LANES * hh:pe0 + LANES * (hh + 1)] * cos
              + qq[:, rh0 + LANES * hh:rh0 + LANES * (hh + 1)] * sin)
        qcat_ref[:, QCAT * hh:QCAT * hh + LANES] = qabs[:, LANES * hh:LANES * (hh + 1)].astype(BF16)
        qcat_ref[:, QCAT * hh + LANES:QCAT * (hh + 1)] = pe.astype(BF16)

    c1 = c0 + Q_RANK
    ckv = _rms(p[:, c1:c1 + KV_RANK], gkv_ref[...])
    ckv_ref[...] = ckv
    kg = p[:, c1 + KV_RANK:c1 + KV_RANK + LANES]
    kpe = kg * cos + pltpu.roll(kg, LANES - MLA_ROPE, 1) * sin
    kpe_ref[...] = kpe[:, :MLA_ROPE]
    kcat_ref[:, :KV_RANK] = ckv.astype(BF16)
    kcat_ref[:, KV_RANK:] = kpe.astype(BF16)


def _inproj(x, shift, scale, g, win, gq, wq, gkv, wuk, cos, sin, *, tm):
    rows, d = x.shape
    mod_rows = shift.shape[1]
    tiles_per_batch = cos.shape[0] // tm
    mod_map = lambda i: (i // tiles_per_batch, 0, 0)
    pos_map = lambda i: (i % tiles_per_batch, 0)
    row_map = lambda i: (i, 0)
    full = lambda i: (0, 0)
    wide = [(SB_WIDTH, BF16), (SB_WIDTH, F32), (SB_WIDTH, F32), (SB_WIDTH, BF16), (SB_WIDTH, BF16),
            (MLA_HEADS * QCAT, BF16), (KV_RANK, F32), (MLA_ROPE, F32), (QCAT, BF16)]
    return pl.pallas_call(
        _inproj_kernel,
        grid=(rows // tm,),
        in_specs=[
            pl.BlockSpec((tm, d), row_map),
            pl.BlockSpec((None, mod_rows, d), mod_map),
            pl.BlockSpec((None, mod_rows, d), mod_map),
            pl.BlockSpec((1, d), full),
            pl.BlockSpec(win.shape, full),
            pl.BlockSpec((1, Q_RANK), full),
            pl.BlockSpec(wq.shape, full),
            pl.BlockSpec((1, KV_RANK), full),
            pl.BlockSpec(wuk.shape, full),
            pl.BlockSpec((tm, LANES), pos_map),
            pl.BlockSpec((tm, LANES), pos_map),
        ],
        out_specs=[pl.BlockSpec((tm, w), row_map) for w, _ in wide],
        out_shape=[jax.ShapeDtypeStruct((rows, w), dt) for w, dt in wide],
        compiler_params=_cp(("parallel",)),
        name="inproj",
    )(x, shift, scale, g, win, gq, wq, gkv, wuk, cos, sin)


def _sb_kernel(q_ref, k_ref, v_ref, u_ref, o_ref, acc_ref, *, tq):
    qi = pl.program_id(2)
    q = q_ref[...].astype(F32)
    lane = lax.broadcasted_iota(jnp.int32, (tq, LANES), 1)
    qm = [jnp.where(lane < SB_HEAD_DIM, q, 0.0).astype(BF16), jnp.where(lane >= SB_HEAD_DIM, q, 0.0).astype(BF16)]
    vis = lax.broadcasted_iota(jnp.int32, (tq, tq), 1) < lax.broadcasted_iota(jnp.int32, (tq, tq), 0)
    u = u_ref[...]

    sub = min(tq, SB_SUB)
    nsub = tq // sub

    def block(kb, runs, diagonal):
        start = pl.multiple_of(kb * tq, tq)
        kblk = k_ref[pl.ds(start, tq), :]
        vblk = v_ref[pl.ds(start, tq), :]
        out = []
        for hh in range(2):
            for rs in range(nsub):
                rows = slice(sub * rs, sub * (rs + 1))
                run = runs[hh * nsub + rs]
                z = _dot_nt(qm[hh][rows], kblk)
                l1m = -_softplus(z)
                if diagonal:
                    l1m = jnp.where(vis[rows], l1m, 0.0)
                hi, lo = _split_bf16(l1m)
                tail = (jnp.dot(hi, u, preferred_element_type=F32)
                        + jnp.dot(lo, u, preferred_element_type=F32)) + run
                a = jnp.exp(z + tail)
                if diagonal:
                    a = jnp.where(vis[rows], a, 0.0)
                acc_ref[hh, rows, :] += jnp.dot(a.astype(BF16), vblk, preferred_element_type=F32)
                out.append(run + jnp.sum(l1m, axis=1, keepdims=True))
        return tuple(out)

    acc_ref[...] = jnp.zeros_like(acc_ref)
    zero = jnp.zeros((sub, 1), F32)
    runs = block(qi, (zero,) * (2 * nsub), True)
    lax.fori_loop(0, qi, lambda i, r: block(qi - 1 - i, r, False), runs)
    o_ref[...] = jnp.where(lane < SB_HEAD_DIM, acc_ref[0], acc_ref[1]).astype(o_ref.dtype)


def _sb_attention(q, k, v, u, *, tq):
    b, t, w = q.shape
    tk = tq
    kern = functools.partial(_sb_kernel, tq=tq)
    return pl.pallas_call(
        kern,
        grid=(b, w // LANES, t // tq),
        in_specs=[
            pl.BlockSpec((None, tq, LANES), lambda bi, hp, qi: (bi, qi, hp)),
            pl.BlockSpec((None, t, LANES), lambda bi, hp, qi: (bi, 0, hp)),
            pl.BlockSpec((None, t, LANES), lambda bi, hp, qi: (bi, 0, hp)),
            pl.BlockSpec((tk, tk), lambda bi, hp, qi: (0, 0)),
        ],
        out_specs=pl.BlockSpec((None, tq, LANES), lambda bi, hp, qi: (bi, qi, hp)),
        out_shape=jax.ShapeDtypeStruct((b, t, w), BF16),
        scratch_shapes=[pltpu.VMEM((2, tq, LANES), F32)],
        compiler_params=_cp(("parallel", "parallel", "parallel")),
        name="sb_attn",
    )(q, k, v, u)


def _mla_kernel(q_ref, k_ref, o_ref, *, tq, tk):
    qi = pl.program_id(1)
    rows = MLA_HEADS * tq
    qs = jnp.concatenate([q_ref[:, QCAT * hh:QCAT * (hh + 1)] for hh in range(MLA_HEADS)], axis=0)
    qpos = qi * tq + (lax.broadcasted_iota(jnp.int32, (rows, tk), 0) & (tq - 1))
    kcol = lax.broadcasted_iota(jnp.int32, (rows, tk), 1)
    nfull = (qi * tq) // tk

    def block(j, carry, masked):
        m, l, acc = carry
        start = pl.multiple_of(j * tk, tk)
        kblk = k_ref[pl.ds(start, tk), :]
        s = _dot_nt(qs, kblk)
        if masked:
            s = jnp.where((j * tk + kcol) <= qpos, s, NEG)
        m_new = jnp.maximum(m, jnp.max(s, axis=1, keepdims=True))
        alpha = jnp.exp((m - m_new) * MLA_SCALE)
        p = jnp.exp((s - m_new) * MLA_SCALE)
        l = alpha * l + jnp.sum(p, axis=1, keepdims=True)
        acc = alpha * acc + jnp.dot(p.astype(BF16), kblk[:, :KV_RANK], preferred_element_type=F32)
        return m_new, l, acc

    init = (jnp.full((rows, 1), NEG, F32), jnp.zeros((rows, 1), F32), jnp.zeros((rows, KV_RANK), F32))
    carry = lax.fori_loop(0, nfull, lambda j, c: block(j, c, False), init)
    _, l, acc = block(nfull, carry, True)
    out = acc / l
    for hh in range(MLA_HEADS):
        o_ref[:, KV_RANK * hh:KV_RANK * (hh + 1)] = out[tq * hh:tq * (hh + 1)].astype(o_ref.dtype)


def _mla_attention(qcat, kcat, *, tq, tk):
    b, t, _ = qcat.shape
    assert tk % tq == 0 and tq & (tq - 1) == 0
    kern = functools.partial(_mla_kernel, tq=tq, tk=tk)
    return pl.pallas_call(
        kern,
        grid=(b, t // tq),
        in_specs=[
            pl.BlockSpec((None, tq, MLA_HEADS * QCAT), lambda bi, qi: (bi, qi, 0)),
            pl.BlockSpec((None, t, QCAT), lambda bi, qi: (bi, 0, 0)),
        ],
        out_specs=pl.BlockSpec((None, tq, MLA_HEADS * KV_RANK), lambda bi, qi: (bi, qi, 0)),
        out_shape=jax.ShapeDtypeStruct((b, t, MLA_HEADS * KV_RANK), BF16),
        compiler_params=_cp(("parallel", "parallel")),
        name="mla_attn",
    )(qcat, kcat)


PAGES = 16
SEG = 256
QROWS = 16


def _sample_attn_kernel(pt_ref, qbd_ref, qcat_ref, knew_ref, vnew_ref, kcnew_ref, u_ref, *rest, past_len):
    pages = rest[:4 * PAGES]
    osb_ref, olat_ref = rest[4 * PAGES:4 * PAGES + 2]
    acc_sb, run_ref, m_ref, l_ref, acc_lat = rest[4 * PAGES + 2:]
    kp, vp, cp, pp = (pages[i * PAGES:(i + 1) * PAGES] for i in range(4))
    c = pl.program_id(1)
    qbd = qbd_ref[...]
    qcat = qcat_ref[...]

    @pl.when(c == 0)
    def _():
        kpos = jnp.full((QROWS, 1), past_len, jnp.int32)
        qpos = jnp.full((QROWS, 1), past_len, jnp.int32)
        z = jnp.sum(qbd.astype(F32) * knew_ref[...].astype(F32), axis=1, keepdims=True)
        vis = kpos < qpos
        l1m = jnp.where(vis, -_softplus(z), 0.0)
        a = jnp.where(vis, jnp.exp(z + l1m), 0.0)
        acc_sb[...] = a * vnew_ref[...].astype(F32)
        run_ref[...] = jnp.broadcast_to(l1m, (QROWS, LANES))
        kc = kcnew_ref[...].astype(F32)
        s = jnp.sum(qcat.astype(F32) * kc, axis=1, keepdims=True) * MLA_SCALE
        vis2 = kpos <= qpos
        m_ref[...] = jnp.broadcast_to(jnp.where(vis2, s, NEG), (QROWS, LANES))
        p = jnp.where(vis2, 1.0, 0.0)
        l_ref[...] = jnp.broadcast_to(p, (QROWS, LANES))
        acc_lat[...] = p * kc[:, :KV_RANK]

    nseg = PAGES * PAGE_SIZE // SEG
    kc = jnp.concatenate([r[...] for r in kp], axis=1).astype(BF16)
    z = jnp.dot(qbd, kc, preferred_element_type=F32)
    l1m = -_softplus(z)
    l4 = jnp.concatenate([l1m[:, SEG * s:SEG * (s + 1)] for s in range(nseg)], axis=0)
    hi, lo = _split_bf16(l4)
    u = u_ref[...]
    cum = jnp.dot(hi, u, preferred_element_type=F32) + jnp.dot(lo, u, preferred_element_type=F32)
    tot = jnp.sum(l4, axis=1, keepdims=True)
    carry = run_ref[:, :1]
    tails = [None] * nseg
    for s in reversed(range(nseg)):
        tails[s] = cum[QROWS * s:QROWS * (s + 1)] + carry
        carry = carry + tot[QROWS * s:QROWS * (s + 1)]
    run_ref[...] = jnp.broadcast_to(carry, (QROWS, LANES))
    a = jnp.exp(z + jnp.concatenate(tails, axis=1))
    vc = jnp.concatenate([r[...] for r in vp], axis=1).astype(BF16)
    acc_sb[...] += _dot_nt(a.astype(BF16), vc)

    ck = jnp.concatenate([r[...] for r in cp], axis=0).astype(BF16)
    pk = jnp.concatenate([r[...] for r in pp], axis=1).astype(BF16)
    s = (_dot_nt(qcat[:, :KV_RANK], ck)
         + jnp.dot(qcat[:, KV_RANK:KV_RANK + MLA_ROPE], pk, preferred_element_type=F32)) * MLA_SCALE
    m_old = m_ref[:, :1]
    m_new = jnp.maximum(m_old, jnp.max(s, axis=1, keepdims=True))
    alpha = jnp.exp(m_old - m_new)
    p = jnp.exp(s - m_new)
    l_new = alpha * l_ref[:, :1] + jnp.sum(p, axis=1, keepdims=True)
    acc_lat[...] = alpha * acc_lat[...] + jnp.dot(p.astype(BF16), ck, preferred_element_type=F32)
    m_ref[...] = jnp.broadcast_to(m_new, (QROWS, LANES))
    l_ref[...] = jnp.broadcast_to(l_new, (QROWS, LANES))

    @pl.when(c == pl.num_programs(1) - 1)
    def _():
        acc = acc_sb[...]
        row = lax.broadcasted_iota(jnp.int32, acc.shape, 0)
        head = lax.broadcasted_iota(jnp.int32, acc.shape, 1) // SB_HEAD_DIM
        osb_ref[...] = jnp.sum(jnp.where(row == head, acc, 0.0), axis=0, keepdims=True).astype(osb_ref.dtype)
        olat_ref[...] = (acc_lat[...] / l_ref[...])[:MLA_HEADS].astype(olat_ref.dtype)


def _sample_attention(layer, page_table, qbd, qcat, knew, vnew, kcnew, u, cache_k, cache_v, cache_c, cache_p):
    nb, n_pages = page_table.shape
    nchunk = n_pages // PAGES
    past_len = n_pages * PAGE_SIZE

    def page_spec(shape, i):
        def imap(b, c, pt):
            return (layer, pt[b, (nchunk - 1 - c) * PAGES + i], 0, 0)
        return pl.BlockSpec((None, None) + shape, imap)

    per_b = lambda b, c, pt: (b, 0, 0)
    in_specs = [
        pl.BlockSpec((None, QROWS, SB_WIDTH), per_b),
        pl.BlockSpec((None, QROWS, QCAT), per_b),
        pl.BlockSpec((None, 1, SB_WIDTH), per_b),
        pl.BlockSpec((None, 1, SB_WIDTH), per_b),
        pl.BlockSpec((None, 1, QCAT), per_b),
        pl.BlockSpec((SEG, SEG), lambda b, c, pt: (0, 0)),
    ]
    operands = [qbd, qcat, knew, vnew, kcnew, u]
    for cache in (cache_k, cache_v, cache_c, cache_p):
        for i in range(PAGES):
            in_specs.append(page_spec(cache.shape[2:], i))
            operands.append(cache)
    grid_spec = pltpu.PrefetchScalarGridSpec(
        num_scalar_prefetch=1,
        grid=(nb, nchunk),
        in_specs=in_specs,
        out_specs=[
            pl.BlockSpec((None, 1, SB_WIDTH), per_b),
            pl.BlockSpec((None, MLA_HEADS, KV_RANK), per_b),
        ],
        scratch_shapes=[
            pltpu.VMEM((QROWS, SB_WIDTH), F32),
            pltpu.VMEM((QROWS, LANES), F32),
            pltpu.VMEM((QROWS, LANES), F32),
            pltpu.VMEM((QROWS, LANES), F32),
            pltpu.VMEM((QROWS, KV_RANK), F32),
        ],
    )
    return pl.pallas_call(
        functools.partial(_sample_attn_kernel, past_len=past_len),
        grid_spec=grid_spec,
        out_shape=[
            jax.ShapeDtypeStruct((nb, 1, SB_WIDTH), BF16),
            jax.ShapeDtypeStruct((nb, MLA_HEADS, KV_RANK), BF16),
        ],
        compiler_params=_cp(("parallel", "arbitrary")),
        name="sample_attn",
    )(page_table, *operands)


def _outproj_kernel(*refs, moe, emit_t):
    (osb_ref, olat_ref, x_ref, gate_ref, wuv_ref, wout_ref, g_ref, shift_ref, scale_ref) = refs[:9]
    if moe and emit_t:
        wr_ref, br_ref, xo_ref, h_ref, comb_ref, combt_ref = refs[9:]
    elif moe:
        wr_ref, br_ref, xo_ref, h_ref, comb_ref = refs[9:]
    else:
        xo_ref, h_ref = refs[9:]
    omla = jnp.dot(olat_ref[...], wuv_ref[...], preferred_element_type=F32)
    o = jnp.concatenate([osb_ref[...], omla.astype(BF16)], axis=1)
    x = x_ref[...] + gate_ref[...] * jnp.dot(o, wout_ref[...], preferred_element_type=F32)
    xo_ref[...] = x
    h = _rms(x, g_ref[...]) * (1.0 + scale_ref[...]) + shift_ref[...]
    h_ref[...] = h.astype(BF16)
    if moe:
        logits = jnp.dot(h.astype(BF16), wr_ref[...].astype(BF16), preferred_element_type=F32) + br_ref[...]
        lane = lax.broadcasted_iota(jnp.int32, logits.shape, 1).astype(F32)
        lg = jnp.where(lane < N_EXPERTS, logits, -jnp.inf)
        m1 = jnp.max(lg, axis=1, keepdims=True)
        i1 = jnp.min(jnp.where(lg == m1, lane, float(LANES)), axis=1, keepdims=True)
        lg2 = jnp.where(lane == i1, -jnp.inf, lg)
        m2 = jnp.max(lg2, axis=1, keepdims=True)
        i2 = jnp.min(jnp.where(lg2 == m2, lane, float(LANES)), axis=1, keepdims=True)
        e = jnp.exp(m2 - m1)
        g1 = 1.0 / (1.0 + e)
        g2 = e / (1.0 + e)
        comb = jnp.where(lane == i1, g1, 0.0) + jnp.where(lane == i2, g2, 0.0)
        comb_ref[...] = comb
        if emit_t:
            combt_ref[...] = comb.T[:N_EXPERTS]


def _outproj(osb, olat, x, gate, wuv, wout, g, shift, scale, router, *, tm, tiles_per_batch, emit_t=False):
    rows, d = x.shape
    mod_rows = gate.shape[1]
    moe = router is not None
    mod_map = lambda i: (i // tiles_per_batch, 0, 0)
    row_map = lambda i: (i, 0)
    full = lambda i: (0, 0)
    in_specs = [
        pl.BlockSpec((tm, SB_WIDTH), row_map),
        pl.BlockSpec((tm, MLA_HEADS * KV_RANK), row_map),
        pl.BlockSpec((tm, d), row_map),
        pl.BlockSpec((None, mod_rows, d), mod_map),
        pl.BlockSpec(wuv.shape, full),
        pl.BlockSpec(wout.shape, full),
        pl.BlockSpec((1, d), full),
        pl.BlockSpec((None, mod_rows, d), mod_map),
        pl.BlockSpec((None, mod_rows, d), mod_map),
    ]
    operands = [osb, olat, x, gate, wuv, wout, g, shift, scale]
    out_specs = [pl.BlockSpec((tm, d), row_map), pl.BlockSpec((tm, d), row_map)]
    out_shape = [jax.ShapeDtypeStruct((rows, d), F32), jax.ShapeDtypeStruct((rows, d), BF16)]
    if moe:
        in_specs += [pl.BlockSpec((d, LANES), full), pl.BlockSpec((1, LANES), full)]
        operands += list(router)
        out_specs.append(pl.BlockSpec((tm, LANES), row_map))
        out_shape.append(jax.ShapeDtypeStruct((rows, LANES), F32))
        if emit_t:
            out_specs.append(pl.BlockSpec((N_EXPERTS, tm), lambda i: (0, i)))
            out_shape.append(jax.ShapeDtypeStruct((N_EXPERTS, rows), F32))
    return pl.pallas_call(
        functools.partial(_outproj_kernel, moe=moe, emit_t=moe and emit_t),
        grid=(rows // tm,),
        in_specs=in_specs,
        out_specs=out_specs,
        out_shape=out_shape,
        compiler_params=_cp(("parallel",)),
        name="outproj",
    )(*operands)


def _ffn_kernel(*refs, moe):
    if moe:
        h_ref, wg_ref, wu_ref, wd_ref, x_ref, gate_ref, comb_ref, o_ref, acc_ref = refs
    else:
        h_ref, wg_ref, wu_ref, wd_ref, x_ref, gate_ref, o_ref, acc_ref = refs
    e = pl.program_id(1)
    j = pl.program_id(2)

    @pl.when((e == 0) & (j == 0))
    def _():
        acc_ref[...] = jnp.zeros_like(acc_ref)

    h = h_ref[...]
    gt = jnp.dot(h, wg_ref[...], preferred_element_type=F32)
    up = jnp.dot(h, wu_ref[...], preferred_element_type=F32)
    act = gt * (1.0 / (1.0 + jnp.exp(-gt))) * up
    y = jnp.dot(act.astype(BF16), wd_ref[...], preferred_element_type=F32)
    if moe:
        comb = comb_ref[...]
        lane = lax.broadcasted_iota(jnp.int32, comb.shape, 1)
        y = jnp.sum(jnp.where(lane == e, comb, 0.0), axis=1, keepdims=True) * y
    acc_ref[...] += y

    @pl.when((e == pl.num_programs(1) - 1) & (j == pl.num_programs(2) - 1))
    def _():
        o_ref[...] = x_ref[...] + gate_ref[...] * acc_ref[...]


def _ffn(h, wg, wu, wd, x, gate, comb, *, tm, tf, tiles_per_batch):
    rows, d = x.shape
    ne, _, f = wg.shape
    mod_rows = gate.shape[1]
    moe = comb is not None
    row_map = lambda i, e, j: (i, 0)
    in_specs = [
        pl.BlockSpec((tm, d), row_map),
        pl.BlockSpec((None, d, tf), lambda i, e, j: (e, 0, j)),
        pl.BlockSpec((None, d, tf), lambda i, e, j: (e, 0, j)),
        pl.BlockSpec((None, tf, d), lambda i, e, j: (e, j, 0)),
        pl.BlockSpec((tm, d), row_map),
        pl.BlockSpec((None, mod_rows, d), lambda i, e, j: (i // tiles_per_batch, 0, 0)),
    ]
    operands = [h, wg, wu, wd, x, gate]
    if moe:
        in_specs.append(pl.BlockSpec((tm, LANES), row_map))
        operands.append(comb)
    return pl.pallas_call(
        functools.partial(_ffn_kernel, moe=moe),
        grid=(rows // tm, ne, f // tf),
        in_specs=in_specs,
        out_specs=pl.BlockSpec((tm, d), row_map),
        out_shape=jax.ShapeDtypeStruct((rows, d), F32),
        scratch_shapes=[pltpu.VMEM((tm, d), F32)],
        compiler_params=_cp(("parallel", "arbitrary", "arbitrary")),
        name="ffn",
    )(*operands)


MOE_TILE = 1024
MOE_CHUNK = 128


def _moe_kernel(h_ref, combt_ref, comb_ref, before_ref, after_ref, wg_ref, wu_ref, wd_ref, x_ref, gate_ref,
                o_ref, xc_ref, gc_ref, yc_ref, acc_ref, rank_ref, nch_ref):
    e = pl.program_id(1)
    j = pl.program_id(2)
    last_j = j == pl.num_programs(2) - 1
    ts, d = h_ref.shape
    ch = MOE_CHUNK

    @pl.when((e == 0) & (j == 0))
    def _():
        acc_ref[...] = jnp.zeros_like(acc_ref)

    @pl.when(j == 0)
    def _():
        row = combt_ref[pl.ds(e, 1), :]
        sel = row > 0.0
        ones = jnp.where(sel, 1.0, 0.0)
        rank = jnp.dot(jnp.broadcast_to(ones, (8, ts)).astype(BF16), before_ref[...],
                       preferred_element_type=F32)[:1]
        rank_ref[...] = jnp.broadcast_to(jnp.where(sel, rank, -1.0), (8, ts))
        nch = (jnp.sum(ones).astype(jnp.int32) + (ch - 1)) // ch
        nch_ref[0] = nch
        h = h_ref[...]
        chi, clo = _split_bf16(comb_ref[...])
        lane = lax.broadcasted_iota(jnp.int32, (ch, LANES), 1)
        slot = lax.broadcasted_iota(jnp.int32, (ch, ts), 0).astype(F32)

        def gather(c, carry):
            base = (c * ch).astype(F32)
            p = jnp.where(rank_ref[:1, :] == slot + base, 1.0, 0.0).astype(BF16)
            xc_ref[c] = jnp.dot(p, h, preferred_element_type=F32).astype(BF16)
            gall = jnp.dot(p, chi, preferred_element_type=F32) + jnp.dot(p, clo, preferred_element_type=F32)
            g = jnp.sum(jnp.where(lane == e, gall, 0.0), axis=1, keepdims=True)
            gc_ref[c] = jnp.broadcast_to(g, (ch, LANES))
            yc_ref[c] = jnp.zeros((ch, d), F32)
            return carry

        lax.fori_loop(0, nch, gather, 0)

    def expert(c, carry):
        xcc = xc_ref[c]
        gt = jnp.dot(xcc, wg_ref[...], preferred_element_type=F32)
        up = jnp.dot(xcc, wu_ref[...], preferred_element_type=F32)
        act = gt * (1.0 / (1.0 + jnp.exp(-gt))) * up
        yc_ref[c] += jnp.dot(act.astype(BF16), wd_ref[...], preferred_element_type=F32)
        return carry

    lax.fori_loop(0, nch_ref[0], expert, 0)

    @pl.when(last_j)
    def _():
        comb = comb_ref[...]
        lane = lax.broadcasted_iota(jnp.int32, comb.shape, 1)
        selm = jnp.where(lane == e, jnp.where(comb > 0.0, 1.0, 0.0), 0.0)
        rank = jnp.sum(jnp.dot(after_ref[...], selm.astype(BF16), preferred_element_type=F32),
                       axis=1, keepdims=True)
        rank = jnp.where(jnp.sum(selm, axis=1, keepdims=True) > 0.0, rank, -1.0)
        slot = lax.broadcasted_iota(jnp.int32, (ts, ch), 1).astype(F32)

        def scatter(c, carry):
            base = (c * ch).astype(F32)
            y = (yc_ref[c] * gc_ref[c][:, :1]).astype(BF16)
            pt = jnp.where(rank == slot + base, 1.0, 0.0).astype(BF16)
            acc_ref[...] += jnp.dot(pt, y, preferred_element_type=F32)
            return carry

        lax.fori_loop(0, nch_ref[0], scatter, 0)

    @pl.when(last_j & (e == pl.num_programs(1) - 1))
    def _():
        o_ref[...] = x_ref[...] + gate_ref[...] * acc_ref[...]


def _moe(h, combt, comb, wg, wu, wd, x, gate, *, ts, tf, tiles_per_batch):
    rows, d = x.shape
    ne, _, f = wg.shape
    tri = jnp.triu(jnp.ones((ts, ts), F32), k=1).astype(BF16)
    row_map = lambda i, e, j: (i, 0)
    full = lambda i, e, j: (0, 0)
    return pl.pallas_call(
        _moe_kernel,
        grid=(rows // ts, ne, f // tf),
        in_specs=[
            pl.BlockSpec((ts, d), row_map),
            pl.BlockSpec((8, ts), lambda i, e, j: (0, i)),
            pl.BlockSpec((ts, LANES), row_map),
            pl.BlockSpec((ts, ts), full, pipeline_mode=pl.Buffered(1)),
            pl.BlockSpec((ts, ts), full, pipeline_mode=pl.Buffered(1)),
            pl.BlockSpec((None, d, tf), lambda i, e, j: (e, 0, j)),
            pl.BlockSpec((None, d, tf), lambda i, e, j: (e, 0, j)),
            pl.BlockSpec((None, tf, d), lambda i, e, j: (e, j, 0)),
            pl.BlockSpec((ts, d), row_map, pipeline_mode=pl.Buffered(1)),
            pl.BlockSpec((None, 1, d), lambda i, e, j: (i // tiles_per_batch, 0, 0)),
        ],
        out_specs=pl.BlockSpec((ts, d), row_map),
        out_shape=jax.ShapeDtypeStruct((rows, d), F32),
        scratch_shapes=[
            pltpu.VMEM((ts // MOE_CHUNK, MOE_CHUNK, d), BF16),
            pltpu.VMEM((ts // MOE_CHUNK, MOE_CHUNK, LANES), F32),
            pltpu.VMEM((ts // MOE_CHUNK, MOE_CHUNK, d), F32),
            pltpu.VMEM((ts, d), F32),
            pltpu.VMEM((8, ts), F32),
            pltpu.SMEM((1,), jnp.int32),
        ],
        compiler_params=_cp(("parallel", "arbitrary", "arbitrary")),
        name="moe",
    )(h, combt, comb, tri, tri.T, wg, wu, wd, x, gate)


def _norm_kernel(x_ref, g_ref, o_ref):
    o_ref[...] = _rms(x_ref[...], g_ref[...])


def _final_norm(x, g, *, tm):
    rows, d = x.shape
    return pl.pallas_call(
        _norm_kernel,
        grid=(rows // tm,),
        in_specs=[pl.BlockSpec((tm, d), lambda i: (i, 0)), pl.BlockSpec((1, d), lambda i: (0, 0))],
        out_specs=pl.BlockSpec((tm, d), lambda i: (i, 0)),
        out_shape=jax.ShapeDtypeStruct((rows, d), F32),
        compiler_params=_cp(("parallel",)),
        name="final_norm",
    )(x, g)


def _rotate_half_cols(w):
    half = MLA_ROPE // 2
    return jnp.concatenate([-w[..., half:], w[..., :half]], axis=-1)


def _prep_layer_weights(w_in, w_q_up, w_uk, w_uv, w_out):
    d = w_in.shape[0]
    c_kpe = 3 * SB_WIDTH + Q_RANK + KV_RANK
    kpe_w = w_in[:, c_kpe:c_kpe + MLA_ROPE]
    win = jnp.concatenate(
        [w_in, _rotate_half_cols(kpe_w), jnp.zeros((d, IN_EXT - c_kpe - 2 * MLA_ROPE), F32)], axis=1).astype(BF16)

    wq = w_q_up.reshape(Q_RANK, MLA_HEADS, MLA_NOPE + MLA_ROPE)
    nope = wq[:, :, :MLA_NOPE].reshape(Q_RANK, MLA_HEADS * MLA_NOPE)
    pe = wq[:, :, MLA_NOPE:]
    pad = jnp.zeros((Q_RANK, MLA_HEADS, LANES - MLA_ROPE), F32)
    pe_ext = jnp.concatenate([pe, pad], axis=2).reshape(Q_RANK, MLA_HEADS * LANES)
    rh_ext = jnp.concatenate([_rotate_half_cols(pe), pad], axis=2).reshape(Q_RANK, MLA_HEADS * LANES)
    wq_all = jnp.concatenate([nope, pe_ext, rh_ext], axis=1).astype(BF16)

    eye = jnp.eye(MLA_HEADS, dtype=F32)
    wuk_bd = jnp.einsum('chn,hg->hngc', w_uk, eye).reshape(MLA_HEADS * MLA_NOPE, MLA_HEADS * KV_RANK).astype(BF16)
    wuv_bd = jnp.einsum('chv,hg->hcgv', w_uv, eye).reshape(MLA_HEADS * KV_RANK, MLA_HEADS * MLA_V).astype(BF16)
    return win, wq_all, wuk_bd, wuv_bd, w_out.astype(BF16)


def _rope_tables(pos):
    inv = ROPE_THETA ** (-jnp.arange(0, MLA_ROPE, 2, dtype=F32) / MLA_ROPE)
    ang = pos.astype(F32)[:, None] * inv[None, :]
    pad = jnp.zeros((pos.shape[0], LANES - MLA_ROPE), F32)
    cos = jnp.concatenate([jnp.cos(ang), jnp.cos(ang), pad], axis=1)
    sin = jnp.concatenate([jnp.sin(ang), jnp.sin(ang), pad], axis=1)
    return cos, sin


def _row_tile(rows, target):
    return min(rows, target)


def kernel(x_prompt, x_sample, c_prompt, c_sample, cache_sb_k, cache_sb_v, cache_mla_ckv, cache_mla_kpe, page_table, w_ada, b_ada, g_mix, g_ffn, w_in, g_q, w_q_up, g_kv, w_uk, w_uv, w_out, w_gate_d, w_up_d, w_down_d, w_router, b_router, w_gate_e, w_up_e, w_down_e, g_final):
    nbp, t, d = x_prompt.shape
    nbs = x_sample.shape[0]
    depth = w_in.shape[0]
    n_pool = cache_sb_k.shape[1]
    past_len = page_table.shape[1] * PAGE_SIZE
    assert x_sample.shape[1] == 1 and page_table.shape[1] % PAGES == 0

    xp = x_prompt.reshape(nbp * t, d)
    xs = x_sample.reshape(nbs, d)
    tm_p = _row_tile(t, 512)
    tpb_p = t // tm_p
    tq_sb = _row_tile(t, 256)
    tq_mla = _row_tile(t, 128)

    mod = _ada(jnp.concatenate([c_prompt, c_sample], axis=0), w_ada, b_ada)
    mod = mod.reshape(depth, nbp + nbs, 6, d)
    mod_p = mod[:, :nbp].transpose(0, 2, 1, 3).reshape(depth, 6, nbp, 1, d)
    mod_s = mod[:, nbp:].transpose(0, 2, 1, 3).reshape(depth, 6, 1, nbs, d)

    cos_p, sin_p = _rope_tables(jnp.arange(t))
    cos_s, sin_s = _rope_tables(jnp.full((nbs,), past_len))
    u_sb = jnp.tril(jnp.ones((tq_sb, tq_sb), F32)).astype(BF16)
    u_seg = jnp.tril(jnp.ones((SEG, SEG), F32)).astype(BF16)

    ck = jnp.transpose(cache_sb_k, (0, 1, 3, 4, 2)).reshape(depth, n_pool, SB_WIDTH, PAGE_SIZE)
    cv = jnp.transpose(cache_sb_v, (0, 1, 3, 4, 2)).reshape(depth, n_pool, SB_WIDTH, PAGE_SIZE)
    cpe = jnp.transpose(cache_mla_kpe, (0, 1, 3, 2))
    head_of_lane = jnp.arange(SB_WIDTH) // SB_HEAD_DIM
    head_mask = (jnp.arange(QROWS)[:, None] == head_of_lane[None, :])

    new_p = ([], [], [], [])
    new_s = ([], [], [], [])
    for layer in range(depth):
        win, wq_all, wuk_bd, wuv_bd, wout = _prep_layer_weights(
            w_in[layer], w_q_up[layer], w_uk[layer], w_uv[layer], w_out[layer])
        gm = g_mix[layer].reshape(1, d)
        gf = g_ffn[layer].reshape(1, d)
        gq = g_q[layer].reshape(1, Q_RANK)
        gkv = g_kv[layer].reshape(1, KV_RANK)
        i = layer // 2
        if layer % 2 == 0:
            router = None
            wg, wu, wd = (w[i][None].astype(BF16) for w in (w_gate_d, w_up_d, w_down_d))
        else:
            wr = jnp.concatenate([w_router[i], jnp.zeros((d, LANES - N_EXPERTS), F32)], axis=1)
            br = jnp.concatenate([b_router[i], jnp.zeros((LANES - N_EXPERTS,), F32)]).reshape(1, LANES)
            router = (wr, br)
            wg, wu, wd = (w[i].astype(BF16) for w in (w_gate_e, w_up_e, w_down_e))

        mp = mod_p[layer]
        sbq, sbk, sbv, kb, vb, qcat, ckv, kpe, kcat = _inproj(
            xp, mp[0], mp[1], gm, win, gq, wq_all, gkv, wuk_bd, cos_p, sin_p, tm=tm_p)
        o_sb = _sb_attention(sbq.reshape(nbp, t, SB_WIDTH), kb.reshape(nbp, t, SB_WIDTH),
                             vb.reshape(nbp, t, SB_WIDTH), u_sb, tq=tq_sb)
        o_lat = _mla_attention(qcat.reshape(nbp, t, MLA_HEADS * QCAT), kcat.reshape(nbp, t, QCAT),
                               tq=tq_mla, tk=min(t, 256))
        res = _outproj(o_sb.reshape(nbp * t, SB_WIDTH), o_lat.reshape(nbp * t, MLA_HEADS * KV_RANK), xp, mp[2],
                       wuv_bd, wout, gf, mp[3], mp[4], router, tm=tm_p, tiles_per_batch=tpb_p, emit_t=True)
        xp, hp = res[0], res[1]
        if router:
            ts_moe = min(MOE_TILE, t)
            xp = _moe(hp, res[3], res[2], wg, wu, wd, xp, mp[5], ts=ts_moe, tf=D_FF // 2,
                      tiles_per_batch=t // ts_moe)
        else:
            xp = _ffn(hp, wg, wu, wd, xp, mp[5], None, tm=tm_p, tf=D_FF // 2, tiles_per_batch=tpb_p)
        for lst, arr in zip(new_p, (sbk, sbv, ckv, kpe)):
            lst.append(arr)

        ms = mod_s[layer]
        sbq, sbk, sbv, kb, vb, qcat, ckv, kpe, kcat = _inproj(
            xs, ms[0], ms[1], gm, win, gq, wq_all, gkv, wuk_bd, cos_s, sin_s, tm=nbs)
        qbd = jnp.where(head_mask[None], sbq[:, None, :], jnp.zeros((), BF16))
        qc = qcat.reshape(nbs, MLA_HEADS, QCAT)
        qc = jnp.concatenate([qc, jnp.zeros((nbs, QROWS - MLA_HEADS, QCAT), BF16)], axis=1)
        o_sb, o_lat = _sample_attention(
            layer, page_table, qbd, qc, kb.reshape(nbs, 1, SB_WIDTH), vb.reshape(nbs, 1, SB_WIDTH),
            kcat.reshape(nbs, 1, QCAT), u_seg, ck, cv, cache_mla_ckv, cpe)
        res = _outproj(o_sb.reshape(nbs, SB_WIDTH), o_lat.reshape(nbs, MLA_HEADS * KV_RANK), xs, ms[2],
                       wuv_bd, wout, gf, ms[3], ms[4], router, tm=nbs, tiles_per_batch=1)
        xs, hs = res[0], res[1]
        xs = _ffn(hs, wg, wu, wd, xs, ms[5], res[2] if router else None, tm=nbs, tf=D_FF // 2, tiles_per_batch=1)
        for lst, arr in zip(new_s, (sbk, sbv, ckv, kpe)):
            lst.append(arr)

    gfin = g_final.reshape(1, d)
    y_prompt = _final_norm(xp, gfin, tm=tm_p).reshape(nbp, t, d)
    y_sample = _final_norm(xs, gfin, tm=nbs).reshape(nbs, 1, d)

    def stack(lst, nb, tt, tail):
        return jnp.stack(lst, 0).reshape((depth, nb, tt) + tail)

    hd = (SB_HEADS, SB_HEAD_DIM)
    return (y_prompt, y_sample,
            stack(new_p[0], nbp, t, hd), stack(new_p[1], nbp, t, hd),
            stack(new_p[2], nbp, t, (KV_RANK,)), stack(new_p[3], nbp, t, (MLA_ROPE,)),
            stack(new_s[0], nbs, 1, hd), stack(new_s[1], nbs, 1, hd),
            stack(new_s[2], nbs, 1, (KV_RANK,)), stack(new_s[3], nbs, 1, (MLA_ROPE,)))
```

```python
import functools
import math

import jax
import jax.numpy as jnp
from jax import lax
from jax.experimental import pallas as pl
from jax.experimental.pallas import tpu as pltpu

F32 = jnp.float32
BF16 = jnp.bfloat16

D_MODEL = 1024
PAGE_SIZE = 128
SB_HEADS = 8
SB_HEAD_DIM = 64
SB_WIDTH = SB_HEADS * SB_HEAD_DIM
SB_SCALE = 1.0 / math.sqrt(SB_HEAD_DIM)
MLA_HEADS = 8
MLA_NOPE = 64
MLA_ROPE = 32
MLA_V = 64
Q_RANK = 256
KV_RANK = 128
MLA_SCALE = 1.0 / math.sqrt(MLA_NOPE + MLA_ROPE)
ROPE_THETA = 10000.0
D_FF = 2816
N_EXPERTS = 8
EPS = 1e-6

LANES = 128
QCAT = 2 * LANES
IN_EXT = 2048
NEG = -1e30

VMEM_LIMIT = 56 * 1024 * 1024


def _cp(sem):
    return pltpu.CompilerParams(dimension_semantics=sem, vmem_limit_bytes=VMEM_LIMIT)


def _rms(x, g):
    return x * lax.rsqrt(jnp.mean(x * x, axis=-1, keepdims=True) + EPS) * g


def _softplus(z):
    return jnp.maximum(z, 0.0) + jnp.log(1.0 + jnp.exp(-jnp.abs(z)))


def _split_bf16(x):
    hi = x.astype(BF16)
    lo = (x - hi.astype(F32)).astype(BF16)
    return hi, lo


def _dot_nt(a, b):
    return lax.dot_general(a, b, (((1,), (1,)), ((), ())), preferred_element_type=F32)


def _ada_kernel(c_ref, w_ref, b_ref, o_ref):
    c = c_ref[...]
    s = c * (1.0 / (1.0 + jnp.exp(-c)))
    o_ref[...] = jnp.dot(s.astype(BF16), w_ref[...].astype(BF16), preferred_element_type=F32) + b_ref[...]


def _ada(c_all, w_ada, b_ada):
    depth, d, n = w_ada.shape
    rows = c_all.shape[0]
    tn = 1536
    return pl.pallas_call(
        _ada_kernel,
        grid=(depth, n // tn),
        in_specs=[
            pl.BlockSpec((rows, d), lambda l, j: (0, 0)),
            pl.BlockSpec((None, d, tn), lambda l, j: (l, 0, j)),
            pl.BlockSpec((None, 1, tn), lambda l, j: (l, 0, j)),
        ],
        out_specs=pl.BlockSpec((None, rows, tn), lambda l, j: (l, 0, j)),
        out_shape=jax.ShapeDtypeStruct((depth, rows, n), F32),
        compiler_params=_cp(("parallel", "parallel")),
        name="ada_mod",
    )(c_all, w_ada, b_ada.reshape(depth, 1, n))


def _inproj_kernel(x_ref, shift_ref, scale_ref, g_ref, win_ref, gq_ref, wq_ref, gkv_ref, wuk_ref,
                   cos_ref, sin_ref,
                   sbq_ref, sbk_ref, sbv_ref, kb_ref, vb_ref, qcat_ref, ckv_ref, kpe_ref, kcat_ref):
    h = _rms(x_ref[...], g_ref[...]) * (1.0 + scale_ref[...]) + shift_ref[...]
    p = jnp.dot(h.astype(BF16), win_ref[...], preferred_element_type=F32)
    sbq_ref[...] = (p[:, :SB_WIDTH] * SB_SCALE).astype(BF16)
    k = p[:, SB_WIDTH:2 * SB_WIDTH]
    v = p[:, 2 * SB_WIDTH:3 * SB_WIDTH]
    sbk_ref[...] = k
    sbv_ref[...] = v
    kb_ref[...] = k.astype(BF16)
    vb_ref[...] = v.astype(BF16)

    c0 = 3 * SB_WIDTH
    qn = _rms(p[:, c0:c0 + Q_RANK], gq_ref[...])
    qq = jnp.dot(qn.astype(BF16), wq_ref[...], preferred_element_type=F32)
    qabs = jnp.dot(qq[:, :SB_WIDTH].astype(BF16), wuk_ref[...], preferred_element_type=F32)
    cos = cos_ref[...]
    sin = sin_ref[...]
    pe0 = MLA_HEADS * MLA_NOPE
    rh0 = pe0 + MLA_HEADS * LANES
    for hh in range(MLA_HEADS):
        pe = (qq[:, pe0 + LANES * hh:pe0 + LANES * (hh + 1)] * cos
              + qq[:, rh0 + LANES * hh:rh0 + LANES * (hh + 1)] * sin)
        qcat_ref[:, QCAT * hh:QCAT * hh + LANES] = qabs[:, LANES * hh:LANES * (hh + 1)].astype(BF16)
        qcat_ref[:, QCAT * hh + LANES:QCAT * (hh + 1)] = pe.astype(BF16)

    c1 = c0 + Q_RANK
    ckv = _rms(p[:, c1:c1 + KV_RANK], gkv_ref[...])
    ckv_ref[...] = ckv
    kg = p[:, c1 + KV_RANK:c1 + KV_RANK + LANES]
    kpe = kg * cos + pltpu.roll(kg, LANES - MLA_ROPE, 1) * sin
    kpe_ref[...] = kpe[:, :MLA_ROPE]
    kcat_ref[:, :KV_RANK] = ckv.astype(BF16)
    kcat_ref[:, KV_RANK:] = kpe.astype(BF16)


def _inproj(x, shift, scale, g, win, gq, wq, gkv, wuk, cos, sin, *, tm):
    rows, d = x.shape
    mod_rows = shift.shape[1]
    tiles_per_batch = cos.shape[0] // tm
    mod_map = lambda i: (i // tiles_per_batch, 0, 0)
    pos_map = lambda i: (i % tiles_per_batch, 0)
    row_map = lambda i: (i, 0)
    full = lambda i: (0, 0)
    wide = [(SB_WIDTH, BF16), (SB_WIDTH, F32), (SB_WIDTH, F32), (SB_WIDTH, BF16), (SB_WIDTH, BF16),
            (MLA_HEADS * QCAT, BF16), (KV_RANK, F32), (MLA_ROPE, F32), (QCAT, BF16)]
    return pl.pallas_call(
        _inproj_kernel,
        grid=(rows // tm,),
        in_specs=[
            pl.BlockSpec((tm, d), row_map),
            pl.BlockSpec((None, mod_rows, d), mod_map),
            pl.BlockSpec((None, mod_rows, d), mod_map),
            pl.BlockSpec((1, d), full),
            pl.BlockSpec(win.shape, full),
            pl.BlockSpec((1, Q_RANK), full),
            pl.BlockSpec(wq.shape, full),
            pl.BlockSpec((1, KV_RANK), full),
            pl.BlockSpec(wuk.shape, full),
            pl.BlockSpec((tm, LANES), pos_map),
            pl.BlockSpec((tm, LANES), pos_map),
        ],
        out_specs=[pl.BlockSpec((tm, w), row_map) for w, _ in wide],
        out_shape=[jax.ShapeDtypeStruct((rows, w), dt) for w, dt in wide],
        compiler_params=_cp(("parallel",)),
        name="inproj",
    )(x, shift, scale, g, win, gq, wq, gkv, wuk, cos, sin)


def _sb_kernel(q_ref, k_ref, v_ref, u_ref, o_ref, acc_ref, *, tq):
    qi = pl.program_id(2)
    q = q_ref[...].astype(F32)
    lane = lax.broadcasted_iota(jnp.int32, (tq, LANES), 1)
    qm = [jnp.where(lane < SB_HEAD_DIM, q, 0.0).astype(BF16), jnp.where(lane >= SB_HEAD_DIM, q, 0.0).astype(BF16)]
    vis = lax.broadcasted_iota(jnp.int32, (tq, tq), 1) < lax.broadcasted_iota(jnp.int32, (tq, tq), 0)
    u = u_ref[...]

    def block(kb, runs, diagonal):
        start = pl.multiple_of(kb * tq, tq)
        kblk = k_ref[pl.ds(start, tq), :]
        vblk = v_ref[pl.ds(start, tq), :]
        out = []
        for hh in range(2):
            z = _dot_nt(qm[hh], kblk)
            l1m = -_softplus(z)
            if diagonal:
                l1m = jnp.where(vis, l1m, 0.0)
            tail = jnp.dot(l1m.astype(BF16), u, preferred_element_type=F32) + runs[hh]
            a = jnp.exp(z + tail)
            if diagonal:
                a = jnp.where(vis, a, 0.0)
            acc_ref[hh] += jnp.dot(a.astype(BF16), vblk, preferred_element_type=F32)
            out.append(runs[hh] + jnp.sum(l1m, axis=1, keepdims=True))
        return tuple(out)

    acc_ref[...] = jnp.zeros_like(acc_ref)
    zero = jnp.zeros((tq, 1), F32)
    runs = block(qi, (zero, zero), True)

    def pair(i, r):
        return block(qi - 2 - 2 * i, block(qi - 1 - 2 * i, r, False), False)

    runs = lax.fori_loop(0, qi // 2, pair, runs)

    @pl.when(qi % 2 == 1)
    def _():
        block(0, runs, False)

    o_ref[...] = jnp.where(lane < SB_HEAD_DIM, acc_ref[0], acc_ref[1]).astype(o_ref.dtype)


def _sb_attention(q, k, v, u, *, tq):
    b, t, w = q.shape
    tk = tq
    kern = functools.partial(_sb_kernel, tq=tq)
    return pl.pallas_call(
        kern,
        grid=(b, w // LANES, t // tq),
        in_specs=[
            pl.BlockSpec((None, tq, LANES), lambda bi, hp, qi: (bi, qi, hp)),
            pl.BlockSpec((None, t, LANES), lambda bi, hp, qi: (bi, 0, hp)),
            pl.BlockSpec((None, t, LANES), lambda bi, hp, qi: (bi, 0, hp)),
            pl.BlockSpec((tk, tk), lambda bi, hp, qi: (0, 0)),
        ],
        out_specs=pl.BlockSpec((None, tq, LANES), lambda bi, hp, qi: (bi, qi, hp)),
        out_shape=jax.ShapeDtypeStruct((b, t, w), BF16),
        scratch_shapes=[pltpu.VMEM((2, tq, LANES), F32)],
        compiler_params=_cp(("parallel", "parallel", "parallel")),
        name="sb_attn",
    )(q, k, v, u)


def _mla_kernel(q_ref, k_ref, o_ref, *, tq, tk):
    qi = pl.program_id(1)
    rows = MLA_HEADS * tq
    qs = jnp.concatenate([q_ref[:, QCAT * hh:QCAT * (hh + 1)] for hh in range(MLA_HEADS)], axis=0)
    qpos = qi * tq + (lax.broadcasted_iota(jnp.int32, (rows, tk), 0) & (tq - 1))
    kcol = lax.broadcasted_iota(jnp.int32, (rows, tk), 1)
    nfull = (qi * tq) // tk

    def block(j, carry, masked):
        m, l, acc = carry
        start = pl.multiple_of(j * tk, tk)
        kblk = k_ref[pl.ds(start, tk), :]
        s = _dot_nt(qs, kblk)
        if masked:
            s = jnp.where((j * tk + kcol) <= qpos, s, NEG)
        m_new = jnp.maximum(m, jnp.max(s, axis=1, keepdims=True))
        alpha = jnp.exp((m - m_new) * MLA_SCALE)
        p = jnp.exp((s - m_new) * MLA_SCALE)
        l = alpha * l + jnp.sum(p, axis=1, keepdims=True)
        acc = alpha * acc + jnp.dot(p.astype(BF16), kblk[:, :KV_RANK], preferred_element_type=F32)
        return m_new, l, acc

    init = (jnp.full((rows, 1), NEG, F32), jnp.zeros((rows, 1), F32), jnp.zeros((rows, KV_RANK), F32))
    carry = lax.fori_loop(0, nfull, lambda j, c: block(j, c, False), init)
    _, l, acc = block(nfull, carry, True)
    out = acc / l
    for hh in range(MLA_HEADS):
        o_ref[:, KV_RANK * hh:KV_RANK * (hh + 1)] = out[tq * hh:tq * (hh + 1)].astype(o_ref.dtype)


def _mla_attention(qcat, kcat, *, tq, tk):
    b, t, _ = qcat.shape
    assert tk % tq == 0 and tq & (tq - 1) == 0
    kern = functools.partial(_mla_kernel, tq=tq, tk=tk)
    return pl.pallas_call(
        kern,
        grid=(b, t // tq),
        in_specs=[
            pl.BlockSpec((None, tq, MLA_HEADS * QCAT), lambda bi, qi: (bi, qi, 0)),
            pl.BlockSpec((None, t, QCAT), lambda bi, qi: (bi, 0, 0)),
        ],
        out_specs=pl.BlockSpec((None, tq, MLA_HEADS * KV_RANK), lambda bi, qi: (bi, qi, 0)),
        out_shape=jax.ShapeDtypeStruct((b, t, MLA_HEADS * KV_RANK), BF16),
        compiler_params=_cp(("parallel", "parallel")),
        name="mla_attn",
    )(qcat, kcat)


PAGES = 16
SEG = 256
QROWS = 16


def _sample_attn_kernel(pt_ref, qbd_ref, qcat_ref, knew_ref, vnew_ref, kcnew_ref, u_ref, *rest, past_len):
    pages = rest[:4 * PAGES]
    osb_ref, olat_ref = rest[4 * PAGES:4 * PAGES + 2]
    acc_sb, run_ref, m_ref, l_ref, acc_lat = rest[4 * PAGES + 2:]
    kp, vp, cp, pp = (pages[i * PAGES:(i + 1) * PAGES] for i in range(4))
    c = pl.program_id(1)
    qbd = qbd_ref[...]
    qcat = qcat_ref[...]

    @pl.when(c == 0)
    def _():
        kpos = jnp.full((QROWS, 1), past_len, jnp.int32)
        qpos = jnp.full((QROWS, 1), past_len, jnp.int32)
        z = jnp.sum(qbd.astype(F32) * knew_ref[...].astype(F32), axis=1, keepdims=True)
        vis = kpos < qpos
        l1m = jnp.where(vis, -_softplus(z), 0.0)
        a = jnp.where(vis, jnp.exp(z + l1m), 0.0)
        acc_sb[...] = a * vnew_ref[...].astype(F32)
        run_ref[...] = jnp.broadcast_to(l1m, (QROWS, LANES))
        kc = kcnew_ref[...].astype(F32)
        s = jnp.sum(qcat.astype(F32) * kc, axis=1, keepdims=True) * MLA_SCALE
        vis2 = kpos <= qpos
        m_ref[...] = jnp.broadcast_to(jnp.where(vis2, s, NEG), (QROWS, LANES))
        p = jnp.where(vis2, 1.0, 0.0)
        l_ref[...] = jnp.broadcast_to(p, (QROWS, LANES))
        acc_lat[...] = p * kc[:, :KV_RANK]

    nseg = PAGES * PAGE_SIZE // SEG
    kc = jnp.concatenate([r[...] for r in kp], axis=1).astype(BF16)
    z = jnp.dot(qbd, kc, preferred_element_type=F32)
    l1m = -_softplus(z)
    l4 = jnp.concatenate([l1m[:, SEG * s:SEG * (s + 1)] for s in range(nseg)], axis=0)
    hi, lo = _split_bf16(l4)
    u = u_ref[...]
    cum = jnp.dot(hi, u, preferred_element_type=F32) + jnp.dot(lo, u, preferred_element_type=F32)
    tot = jnp.sum(l4, axis=1, keepdims=True)
    carry = run_ref[:, :1]
    tails = [None] * nseg
    for s in reversed(range(nseg)):
        tails[s] = cum[QROWS * s:QROWS * (s + 1)] + carry
        carry = carry + tot[QROWS * s:QROWS * (s + 1)]
    run_ref[...] = jnp.broadcast_to(carry, (QROWS, LANES))
    a = jnp.exp(z + jnp.concatenate(tails, axis=1))
    vc = jnp.concatenate([r[...] for r in vp], axis=1).astype(BF16)
    acc_sb[...] += _dot_nt(a.astype(BF16), vc)

    ck = jnp.concatenate([r[...] for r in cp], axis=0).astype(BF16)
    pk = jnp.concatenate([r[...] for r in pp], axis=1).astype(BF16)
    s = (_dot_nt(qcat[:, :KV_RANK], ck)
         + jnp.dot(qcat[:, KV_RANK:KV_RANK + MLA_ROPE], pk, preferred_element_type=F32)) * MLA_SCALE
    m_old = m_ref[:, :1]
    m_new = jnp.maximum(m_old, jnp.max(s, axis=1, keepdims=True))
    alpha = jnp.exp(m_old - m_new)
    p = jnp.exp(s - m_new)
    l_new = alpha * l_ref[:, :1] + jnp.sum(p, axis=1, keepdims=True)
    acc_lat[...] = alpha * acc_lat[...] + jnp.dot(p.astype(BF16), ck, preferred_element_type=F32)
    m_ref[...] = jnp.broadcast_to(m_new, (QROWS, LANES))
    l_ref[...] = jnp.broadcast_to(l_new, (QROWS, LANES))

    @pl.when(c == pl.num_programs(1) - 1)
    def _():
        acc = acc_sb[...]
        row = lax.broadcasted_iota(jnp.int32, acc.shape, 0)
        head = lax.broadcasted_iota(jnp.int32, acc.shape, 1) // SB_HEAD_DIM
        osb_ref[...] = jnp.sum(jnp.where(row == head, acc, 0.0), axis=0, keepdims=True).astype(osb_ref.dtype)
        olat_ref[...] = (acc_lat[...] / l_ref[...])[:MLA_HEADS].astype(olat_ref.dtype)


def _sample_attention(layer, page_table, qbd, qcat, knew, vnew, kcnew, u, cache_k, cache_v, cache_c, cache_p):
    nb, n_pages = page_table.shape
    nchunk = n_pages // PAGES
    past_len = n_pages * PAGE_SIZE

    def page_spec(shape, i):
        def imap(b, c, pt):
            return (layer, pt[b, (nchunk - 1 - c) * PAGES + i], 0, 0)
        return pl.BlockSpec((None, None) + shape, imap)

    per_b = lambda b, c, pt: (b, 0, 0)
    in_specs = [
        pl.BlockSpec((None, QROWS, SB_WIDTH), per_b),
        pl.BlockSpec((None, QROWS, QCAT), per_b),
        pl.BlockSpec((None, 1, SB_WIDTH), per_b),
        pl.BlockSpec((None, 1, SB_WIDTH), per_b),
        pl.BlockSpec((None, 1, QCAT), per_b),
        pl.BlockSpec((SEG, SEG), lambda b, c, pt: (0, 0)),
    ]
    operands = [qbd, qcat, knew, vnew, kcnew, u]
    for cache in (cache_k, cache_v, cache_c, cache_p):
        for i in range(PAGES):
            in_specs.append(page_spec(cache.shape[2:], i))
            operands.append(cache)
    grid_spec = pltpu.PrefetchScalarGridSpec(
        num_scalar_prefetch=1,
        grid=(nb, nchunk),
        in_specs=in_specs,
        out_specs=[
            pl.BlockSpec((None, 1, SB_WIDTH), per_b),
            pl.BlockSpec((None, MLA_HEADS, KV_RANK), per_b),
        ],
        scratch_shapes=[
            pltpu.VMEM((QROWS, SB_WIDTH), F32),
            pltpu.VMEM((QROWS, LANES), F32),
            pltpu.VMEM((QROWS, LANES), F32),
            pltpu.VMEM((QROWS, LANES), F32),
            pltpu.VMEM((QROWS, KV_RANK), F32),
        ],
    )
    return pl.pallas_call(
        functools.partial(_sample_attn_kernel, past_len=past_len),
        grid_spec=grid_spec,
        out_shape=[
            jax.ShapeDtypeStruct((nb, 1, SB_WIDTH), BF16),
            jax.ShapeDtypeStruct((nb, MLA_HEADS, KV_RANK), BF16),
        ],
        compiler_params=_cp(("parallel", "arbitrary")),
        name="sample_attn",
    )(page_table, *operands)


def _outproj_kernel(*refs, moe, emit_t):
    (osb_ref, olat_ref, x_ref, gate_ref, wuv_ref, wout_ref, g_ref, shift_ref, scale_ref) = refs[:9]
    if moe and emit_t:
        wr_ref, br_ref, xo_ref, h_ref, comb_ref, combt_ref = refs[9:]
    elif moe:
        wr_ref, br_ref, xo_ref, h_ref, comb_ref = refs[9:]
    else:
        xo_ref, h_ref = refs[9:]
    omla = jnp.dot(olat_ref[...], wuv_ref[...], preferred_element_type=F32)
    o = jnp.concatenate([osb_ref[...], omla.astype(BF16)], axis=1)
    x = x_ref[...] + gate_ref[...] * jnp.dot(o, wout_ref[...], preferred_element_type=F32)
    xo_ref[...] = x
    h = _rms(x, g_ref[...]) * (1.0 + scale_ref[...]) + shift_ref[...]
    h_ref[...] = h.astype(BF16)
    if moe:
        logits = jnp.dot(h.astype(BF16), wr_ref[...].astype(BF16), preferred_element_type=F32) + br_ref[...]
        lane = lax.broadcasted_iota(jnp.int32, logits.shape, 1).astype(F32)
        lg = jnp.where(lane < N_EXPERTS, logits, -jnp.inf)
        m1 = jnp.max(lg, axis=1, keepdims=True)
        i1 = jnp.min(jnp.where(lg == m1, lane, float(LANES)), axis=1, keepdims=True)
        lg2 = jnp.where(lane == i1, -jnp.inf, lg)
        m2 = jnp.max(lg2, axis=1, keepdims=True)
        i2 = jnp.min(jnp.where(lg2 == m2, lane, float(LANES)), axis=1, keepdims=True)
        e = jnp.exp(m2 - m1)
        g1 = 1.0 / (1.0 + e)
        g2 = e / (1.0 + e)
        comb = jnp.where(lane == i1, g1, 0.0) + jnp.where(lane == i2, g2, 0.0)
        comb_ref[...] = comb
        if emit_t:
            combt_ref[...] = comb.T[:N_EXPERTS]


def _outproj(osb, olat, x, gate, wuv, wout, g, shift, scale, router, *, tm, tiles_per_batch, emit_t=False):
    rows, d = x.shape
    mod_rows = gate.shape[1]
    moe = router is not None
    mod_map = lambda i: (i // tiles_per_batch, 0, 0)
    row_map = lambda i: (i, 0)
    full = lambda i: (0, 0)
    in_specs = [
        pl.BlockSpec((tm, SB_WIDTH), row_map),
        pl.BlockSpec((tm, MLA_HEADS * KV_RANK), row_map),
        pl.BlockSpec((tm, d), row_map),
        pl.BlockSpec((None, mod_rows, d), mod_map),
        pl.BlockSpec(wuv.shape, full),
        pl.BlockSpec(wout.shape, full),
        pl.BlockSpec((1, d), full),
        pl.BlockSpec((None, mod_rows, d), mod_map),
        pl.BlockSpec((None, mod_rows, d), mod_map),
    ]
    operands = [osb, olat, x, gate, wuv, wout, g, shift, scale]
    out_specs = [pl.BlockSpec((tm, d), row_map), pl.BlockSpec((tm, d), row_map)]
    out_shape = [jax.ShapeDtypeStruct((rows, d), F32), jax.ShapeDtypeStruct((rows, d), BF16)]
    if moe:
        in_specs += [pl.BlockSpec((d, LANES), full), pl.BlockSpec((1, LANES), full)]
        operands += list(router)
        out_specs.append(pl.BlockSpec((tm, LANES), row_map))
        out_shape.append(jax.ShapeDtypeStruct((rows, LANES), F32))
        if emit_t:
            out_specs.append(pl.BlockSpec((N_EXPERTS, tm), lambda i: (0, i)))
            out_shape.append(jax.ShapeDtypeStruct((N_EXPERTS, rows), F32))
    return pl.pallas_call(
        functools.partial(_outproj_kernel, moe=moe, emit_t=moe and emit_t),
        grid=(rows // tm,),
        in_specs=in_specs,
        out_specs=out_specs,
        out_shape=out_shape,
        compiler_params=_cp(("parallel",)),
        name="outproj",
    )(*operands)


def _ffn_kernel(*refs, moe):
    if moe:
        h_ref, wg_ref, wu_ref, wd_ref, x_ref, gate_ref, comb_ref, o_ref, acc_ref = refs
    else:
        h_ref, wg_ref, wu_ref, wd_ref, x_ref, gate_ref, o_ref, acc_ref = refs
    e = pl.program_id(1)
    j = pl.program_id(2)

    @pl.when((e == 0) & (j == 0))
    def _():
        acc_ref[...] = jnp.zeros_like(acc_ref)

    h = h_ref[...]
    gt = jnp.dot(h, wg_ref[...], preferred_element_type=F32)
    up = jnp.dot(h, wu_ref[...], preferred_element_type=F32)
    act = gt * (1.0 / (1.0 + jnp.exp(-gt))) * up
    y = jnp.dot(act.astype(BF16), wd_ref[...], preferred_element_type=F32)
    if moe:
        comb = comb_ref[...]
        lane = lax.broadcasted_iota(jnp.int32, comb.shape, 1)
        y = jnp.sum(jnp.where(lane == e, comb, 0.0), axis=1, keepdims=True) * y
    acc_ref[...] += y

    @pl.when((e == pl.num_programs(1) - 1) & (j == pl.num_programs(2) - 1))
    def _():
        o_ref[...] = x_ref[...] + gate_ref[...] * acc_ref[...]


def _ffn(h, wg, wu, wd, li, x, gate, comb, *, tm, tf, tiles_per_batch):
    rows, d = x.shape
    _, ne, _, f = wg.shape
    mod_rows = gate.shape[1]
    moe = comb is not None
    row_map = lambda i, e, j: (i, 0)
    in_specs = [
        pl.BlockSpec((tm, d), row_map),
        pl.BlockSpec((None, None, d, tf), lambda i, e, j: (li, e, 0, j)),
        pl.BlockSpec((None, None, d, tf), lambda i, e, j: (li, e, 0, j)),
        pl.BlockSpec((None, None, tf, d), lambda i, e, j: (li, e, j, 0)),
        pl.BlockSpec((tm, d), row_map),
        pl.BlockSpec((None, mod_rows, d), lambda i, e, j: (i // tiles_per_batch, 0, 0)),
    ]
    operands = [h, wg, wu, wd, x, gate]
    if moe:
        in_specs.append(pl.BlockSpec((tm, LANES), row_map))
        operands.append(comb)
    return pl.pallas_call(
        functools.partial(_ffn_kernel, moe=moe),
        grid=(rows // tm, ne, f // tf),
        in_specs=in_specs,
        out_specs=pl.BlockSpec((tm, d), row_map),
        out_shape=jax.ShapeDtypeStruct((rows, d), F32),
        scratch_shapes=[pltpu.VMEM((tm, d), F32)],
        compiler_params=_cp(("parallel", "arbitrary", "arbitrary")),
        name="ffn",
    )(*operands)


MOE_TILE = 1024
MOE_CHUNK = 288


def _moe_kernel(h_ref, combt_ref, comb_ref, before_ref, after_ref, wg_ref, wu_ref, wd_ref, x_ref, gate_ref,
                o_ref, xc_ref, gc_ref, yc_ref, acc_ref, rankrow_ref, rankcol_ref, nch_ref):
    e = pl.program_id(1)
    j = pl.program_id(2)
    last_j = j == pl.num_programs(2) - 1
    ts, d = h_ref.shape
    ch = MOE_CHUNK

    @pl.when((e == 0) & (j == 0))
    def _():
        acc_ref[...] = jnp.zeros_like(acc_ref)
        wt = combt_ref[...]
        cnt_before = jnp.dot(jnp.where(wt > 0.0, 1.0, 0.0).astype(BF16), before_ref[...],
                             preferred_element_type=F32)
        rankrow_ref[...] = jnp.where(wt > 0.0, cnt_before, -1.0)
        w = comb_ref[...]
        cnt_after = jnp.dot(after_ref[...], jnp.where(w > 0.0, 1.0, 0.0).astype(BF16),
                            preferred_element_type=F32)
        rankcol_ref[...] = jnp.where(w > 0.0, cnt_after, -1.0)

    @pl.when(j == 0)
    def _():
        rank_row = rankrow_ref[pl.ds(e, 1), :]
        nch = (jnp.sum(jnp.where(rank_row >= 0.0, 1.0, 0.0)).astype(jnp.int32) + (ch - 1)) // ch
        nch_ref[0] = nch
        h = h_ref[...]
        chi, clo = _split_bf16(comb_ref[...])
        lane = lax.broadcasted_iota(jnp.int32, (ch, LANES), 1)
        slot = lax.broadcasted_iota(jnp.int32, (ch, ts), 0).astype(F32)

        def gather(c, carry):
            base = (c * ch).astype(F32)
            p = jnp.where(rank_row == slot + base, 1.0, 0.0).astype(BF16)
            xc_ref[c] = jnp.dot(p, h, preferred_element_type=F32).astype(BF16)
            gall = jnp.dot(p, chi, preferred_element_type=F32) + jnp.dot(p, clo, preferred_element_type=F32)
            g = jnp.sum(jnp.where(lane == e, gall, 0.0), axis=1, keepdims=True)
            gc_ref[c] = jnp.broadcast_to(g, (ch, LANES))
            yc_ref[c] = jnp.zeros((ch, d), F32)
            return carry

        lax.fori_loop(0, nch, gather, 0)

    def expert(c, carry):
        xcc = xc_ref[c]
        gt = jnp.dot(xcc, wg_ref[...], preferred_element_type=F32)
        up = jnp.dot(xcc, wu_ref[...], preferred_element_type=F32)
        act = gt * (1.0 / (1.0 + jnp.exp(-gt))) * up
        yc_ref[c] += jnp.dot(act.astype(BF16), wd_ref[...], preferred_element_type=F32)
        return carry

    lax.fori_loop(0, nch_ref[0], expert, 0)

    @pl.when(last_j)
    def _():
        lane = lax.broadcasted_iota(jnp.int32, (ts, LANES), 1)
        rank = jnp.sum(jnp.where(lane == e, rankcol_ref[...], 0.0), axis=1, keepdims=True)
        slot = lax.broadcasted_iota(jnp.int32, (ts, ch), 1).astype(F32)

        def scatter(c, carry):
            base = (c * ch).astype(F32)
            y = (yc_ref[c] * gc_ref[c][:, :1]).astype(BF16)
            pt = jnp.where(rank == slot + base, 1.0, 0.0).astype(BF16)
            acc_ref[...] += jnp.dot(pt, y, preferred_element_type=F32)
            return carry

        lax.fori_loop(0, nch_ref[0], scatter, 0)

    @pl.when(last_j & (e == pl.num_programs(1) - 1))
    def _():
        o_ref[...] = x_ref[...] + gate_ref[...] * acc_ref[...]


def _moe(h, combt, comb, wg, wu, wd, li, x, gate, *, ts, tf, tiles_per_batch):
    rows, d = x.shape
    _, ne, _, f = wg.shape
    tri = jnp.triu(jnp.ones((ts, ts), F32), k=1).astype(BF16)
    row_map = lambda i, e, j: (i, 0)
    full = lambda i, e, j: (0, 0)
    return pl.pallas_call(
        _moe_kernel,
        grid=(rows // ts, ne, f // tf),
        in_specs=[
            pl.BlockSpec((ts, d), row_map),
            pl.BlockSpec((8, ts), lambda i, e, j: (0, i)),
            pl.BlockSpec((ts, LANES), row_map),
            pl.BlockSpec((ts, ts), full, pipeline_mode=pl.Buffered(1)),
            pl.BlockSpec((ts, ts), full, pipeline_mode=pl.Buffered(1)),
            pl.BlockSpec((None, None, d, tf), lambda i, e, j: (li, e, 0, j)),
            pl.BlockSpec((None, None, d, tf), lambda i, e, j: (li, e, 0, j)),
            pl.BlockSpec((None, None, tf, d), lambda i, e, j: (li, e, j, 0)),
            pl.BlockSpec((ts, d), row_map, pipeline_mode=pl.Buffered(1)),
            pl.BlockSpec((None, 1, d), lambda i, e, j: (i // tiles_per_batch, 0, 0)),
        ],
        out_specs=pl.BlockSpec((ts, d), row_map),
        out_shape=jax.ShapeDtypeStruct((rows, d), F32),
        scratch_shapes=[
            pltpu.VMEM((pl.cdiv(ts, MOE_CHUNK), MOE_CHUNK, d), BF16),
            pltpu.VMEM((pl.cdiv(ts, MOE_CHUNK), MOE_CHUNK, LANES), F32),
            pltpu.VMEM((pl.cdiv(ts, MOE_CHUNK), MOE_CHUNK, d), F32),
            pltpu.VMEM((ts, d), F32),
            pltpu.VMEM((8, ts), F32),
            pltpu.VMEM((ts, LANES), F32),
            pltpu.SMEM((1,), jnp.int32),
        ],
        compiler_params=_cp(("parallel", "arbitrary", "arbitrary")),
        name="moe",
    )(h, combt, comb, tri, tri.T, wg, wu, wd, x, gate)


def _norm_kernel(x_ref, g_ref, o_ref):
    o_ref[...] = _rms(x_ref[...], g_ref[...])


def _final_norm(x, g, *, tm):
    rows, d = x.shape
    return pl.pallas_call(
        _norm_kernel,
        grid=(rows // tm,),
        in_specs=[pl.BlockSpec((tm, d), lambda i: (i, 0)), pl.BlockSpec((1, d), lambda i: (0, 0))],
        out_specs=pl.BlockSpec((tm, d), lambda i: (i, 0)),
        out_shape=jax.ShapeDtypeStruct((rows, d), F32),
        compiler_params=_cp(("parallel",)),
        name="final_norm",
    )(x, g)


def _rotate_half_cols(w):
    half = MLA_ROPE // 2
    return jnp.concatenate([-w[..., half:], w[..., :half]], axis=-1)


def _prep_layer_weights(w_in, w_q_up, w_uk, w_uv, w_out):
    d = w_in.shape[0]
    c_kpe = 3 * SB_WIDTH + Q_RANK + KV_RANK
    kpe_w = w_in[:, c_kpe:c_kpe + MLA_ROPE]
    win = jnp.concatenate(
        [w_in, _rotate_half_cols(kpe_w), jnp.zeros((d, IN_EXT - c_kpe - 2 * MLA_ROPE), F32)], axis=1).astype(BF16)

    wq = w_q_up.reshape(Q_RANK, MLA_HEADS, MLA_NOPE + MLA_ROPE)
    nope = wq[:, :, :MLA_NOPE].reshape(Q_RANK, MLA_HEADS * MLA_NOPE)
    pe = wq[:, :, MLA_NOPE:]
    pad = jnp.zeros((Q_RANK, MLA_HEADS, LANES - MLA_ROPE), F32)
    pe_ext = jnp.concatenate([pe, pad], axis=2).reshape(Q_RANK, MLA_HEADS * LANES)
    rh_ext = jnp.concatenate([_rotate_half_cols(pe), pad], axis=2).reshape(Q_RANK, MLA_HEADS * LANES)
    wq_all = jnp.concatenate([nope, pe_ext, rh_ext], axis=1).astype(BF16)

    eye = jnp.eye(MLA_HEADS, dtype=F32)
    wuk_bd = jnp.einsum('chn,hg->hngc', w_uk, eye).reshape(MLA_HEADS * MLA_NOPE, MLA_HEADS * KV_RANK).astype(BF16)
    wuv_bd = jnp.einsum('chv,hg->hcgv', w_uv, eye).reshape(MLA_HEADS * KV_RANK, MLA_HEADS * MLA_V).astype(BF16)
    return win, wq_all, wuk_bd, wuv_bd, w_out.astype(BF16)


def _rope_tables(pos):
    inv = ROPE_THETA ** (-jnp.arange(0, MLA_ROPE, 2, dtype=F32) / MLA_ROPE)
    ang = pos.astype(F32)[:, None] * inv[None, :]
    pad = jnp.zeros((pos.shape[0], LANES - MLA_ROPE), F32)
    cos = jnp.concatenate([jnp.cos(ang), jnp.cos(ang), pad], axis=1)
    sin = jnp.concatenate([jnp.sin(ang), jnp.sin(ang), pad], axis=1)
    return cos, sin


def _row_tile(rows, target):
    return min(rows, target)


def kernel(x_prompt, x_sample, c_prompt, c_sample, cache_sb_k, cache_sb_v, cache_mla_ckv, cache_mla_kpe, page_table, w_ada, b_ada, g_mix, g_ffn, w_in, g_q, w_q_up, g_kv, w_uk, w_uv, w_out, w_gate_d, w_up_d, w_down_d, w_router, b_router, w_gate_e, w_up_e, w_down_e, g_final):
    nbp, t, d = x_prompt.shape
    nbs = x_sample.shape[0]
    depth = w_in.shape[0]
    n_pool = cache_sb_k.shape[1]
    past_len = page_table.shape[1] * PAGE_SIZE
    assert x_sample.shape[1] == 1 and page_table.shape[1] % PAGES == 0

    xp = x_prompt.reshape(nbp * t, d)
    xs = x_sample.reshape(nbs, d)
    tm_p = _row_tile(t, 512)
    tpb_p = t // tm_p
    tq_sb = _row_tile(t, 256)
    tq_mla = _row_tile(t, 128)

    mod = _ada(jnp.concatenate([c_prompt, c_sample], axis=0), w_ada, b_ada)
    mod = mod.reshape(depth, nbp + nbs, 6, d)
    mod_p = mod[:, :nbp].transpose(0, 2, 1, 3).reshape(depth, 6, nbp, 1, d)
    mod_s = mod[:, nbp:].transpose(0, 2, 1, 3).reshape(depth, 6, 1, nbs, d)

    cos_p, sin_p = _rope_tables(jnp.arange(t))
    cos_s, sin_s = _rope_tables(jnp.full((nbs,), past_len))
    u_sb = jnp.tril(jnp.ones((tq_sb, tq_sb), F32)).astype(BF16)
    u_seg = jnp.tril(jnp.ones((SEG, SEG), F32)).astype(BF16)

    ck = jnp.transpose(cache_sb_k, (0, 1, 3, 4, 2)).reshape(depth, n_pool, SB_WIDTH, PAGE_SIZE)
    cv = jnp.transpose(cache_sb_v, (0, 1, 3, 4, 2)).reshape(depth, n_pool, SB_WIDTH, PAGE_SIZE)
    cpe = jnp.transpose(cache_mla_kpe, (0, 1, 3, 2))
    head_of_lane = jnp.arange(SB_WIDTH) // SB_HEAD_DIM
    head_mask = (jnp.arange(QROWS)[:, None] == head_of_lane[None, :])

    ffn_dense = tuple(w.astype(BF16).reshape((w.shape[0], 1) + w.shape[1:]) for w in (w_gate_d, w_up_d, w_down_d))
    ffn_experts = tuple(w.astype(BF16) for w in (w_gate_e, w_up_e, w_down_e))

    new_p = ([], [], [], [])
    new_s = ([], [], [], [])
    for layer in range(depth):
        win, wq_all, wuk_bd, wuv_bd, wout = _prep_layer_weights(
            w_in[layer], w_q_up[layer], w_uk[layer], w_uv[layer], w_out[layer])
        gm = g_mix[layer].reshape(1, d)
        gf = g_ffn[layer].reshape(1, d)
        gq = g_q[layer].reshape(1, Q_RANK)
        gkv = g_kv[layer].reshape(1, KV_RANK)
        i = layer // 2
        if layer % 2 == 0:
            router = None
            wg, wu, wd = ffn_dense
        else:
            wr = jnp.concatenate([w_router[i], jnp.zeros((d, LANES - N_EXPERTS), F32)], axis=1)
            br = jnp.concatenate([b_router[i], jnp.zeros((LANES - N_EXPERTS,), F32)]).reshape(1, LANES)
            router = (wr, br)
            wg, wu, wd = ffn_experts

        mp = mod_p[layer]
        sbq, sbk, sbv, kb, vb, qcat, ckv, kpe, kcat = _inproj(
            xp, mp[0], mp[1], gm, win, gq, wq_all, gkv, wuk_bd, cos_p, sin_p, tm=tm_p)
        o_sb = _sb_attention(sbq.reshape(nbp, t, SB_WIDTH), kb.reshape(nbp, t, SB_WIDTH),
                             vb.reshape(nbp, t, SB_WIDTH), u_sb, tq=tq_sb)
        o_lat = _mla_attention(qcat.reshape(nbp, t, MLA_HEADS * QCAT), kcat.reshape(nbp, t, QCAT),
                               tq=tq_mla, tk=min(t, 256))
        res = _outproj(o_sb.reshape(nbp * t, SB_WIDTH), o_lat.reshape(nbp * t, MLA_HEADS * KV_RANK), xp, mp[2],
                       wuv_bd, wout, gf, mp[3], mp[4], router, tm=tm_p, tiles_per_batch=tpb_p, emit_t=True)
        xp, hp = res[0], res[1]
        if router:
            ts_moe = min(MOE_TILE, t)
            xp = _moe(hp, res[3], res[2], wg, wu, wd, i, xp, mp[5], ts=ts_moe, tf=D_FF // 2,
                      tiles_per_batch=t // ts_moe)
        else:
            xp = _ffn(hp, wg, wu, wd, i, xp, mp[5], None, tm=tm_p, tf=D_FF // 2, tiles_per_batch=tpb_p)
        for lst, arr in zip(new_p, (sbk, sbv, ckv, kpe)):
            lst.append(arr)

        ms = mod_s[layer]
        sbq, sbk, sbv, kb, vb, qcat, ckv, kpe, kcat = _inproj(
            xs, ms[0], ms[1], gm, win, gq, wq_all, gkv, wuk_bd, cos_s, sin_s, tm=nbs)
        qbd = jnp.where(head_mask[None], sbq[:, None, :], jnp.zeros((), BF16))
        qc = qcat.reshape(nbs, MLA_HEADS, QCAT)
        qc = jnp.concatenate([qc, jnp.zeros((nbs, QROWS - MLA_HEADS, QCAT), BF16)], axis=1)
        o_sb, o_lat = _sample_attention(
            layer, page_table, qbd, qc, kb.reshape(nbs, 1, SB_WIDTH), vb.reshape(nbs, 1, SB_WIDTH),
            kcat.reshape(nbs, 1, QCAT), u_seg, ck, cv, cache_mla_ckv, cpe)
        res = _outproj(o_sb.reshape(nbs, SB_WIDTH), o_lat.reshape(nbs, MLA_HEADS * KV_RANK), xs, ms[2],
                       wuv_bd, wout, gf, ms[3], ms[4], router, tm=nbs, tiles_per_batch=1)
        xs, hs = res[0], res[1]
        xs = _ffn(hs, wg, wu, wd, i, xs, ms[5], res[2] if router else None, tm=nbs, tf=D_FF // 2,
                  tiles_per_batch=1)
        for lst, arr in zip(new_s, (sbk, sbv, ckv, kpe)):
            lst.append(arr)

    gfin = g_final.reshape(1, d)
    y_prompt = _final_norm(xp, gfin, tm=tm_p).reshape(nbp, t, d)
    y_sample = _final_norm(xs, gfin, tm=nbs).reshape(nbs, 1, d)

    def stack(lst, nb, tt, tail):
        return jnp.stack(lst, 0).reshape((depth, nb, tt) + tail)

    hd = (SB_HEADS, SB_HEAD_DIM)
    return (y_prompt, y_sample,
            stack(new_p[0], nbp, t, hd), stack(new_p[1], nbp, t, hd),
            stack(new_p[2], nbp, t, (KV_RANK,)), stack(new_p[3], nbp, t, (MLA_ROPE,)),
            stack(new_s[0], nbs, 1, hd), stack(new_s[1], nbs, 1, hd),
            stack(new_s[2], nbs, 1, (KV_RANK,)), stack(new_s[3], nbs, 1, (MLA_ROPE,)))
```

```python
import functools
import math

import jax
import jax.numpy as jnp
from jax import lax
from jax.experimental import pallas as pl
from jax.experimental.pallas import tpu as pltpu

F32 = jnp.float32
BF16 = jnp.bfloat16

D_MODEL = 1024
PAGE_SIZE = 128
SB_HEADS = 8
SB_HEAD_DIM = 64
SB_WIDTH = SB_HEADS * SB_HEAD_DIM
SB_SCALE = 1.0 / math.sqrt(SB_HEAD_DIM)
MLA_HEADS = 8
MLA_NOPE = 64
MLA_ROPE = 32
MLA_V = 64
Q_RANK = 256
KV_RANK = 128
MLA_SCALE = 1.0 / math.sqrt(MLA_NOPE + MLA_ROPE)
ROPE_THETA = 10000.0
D_FF = 2816
N_EXPERTS = 8
EPS = 1e-6

LANES = 128
QCAT = 2 * LANES
IN_EXT = 2048
NEG = -1e30

VMEM_LIMIT = 56 * 1024 * 1024


def _cp(sem):
    return pltpu.CompilerParams(dimension_semantics=sem, vmem_limit_bytes=VMEM_LIMIT)


def _rms(x, g):
    return x * lax.rsqrt(jnp.mean(x * x, axis=-1, keepdims=True) + EPS) * g


def _softplus(z):
    return jnp.maximum(z, 0.0) + jnp.log(1.0 + jnp.exp(-jnp.abs(z)))


def _split_bf16(x):
    hi = x.astype(BF16)
    lo = (x - hi.astype(F32)).astype(BF16)
    return hi, lo


def _dot_nt(a, b):
    return lax.dot_general(a, b, (((1,), (1,)), ((), ())), preferred_element_type=F32)


def _ada_kernel(c_ref, w_ref, b_ref, o_ref):
    c = c_ref[...]
    s = c * (1.0 / (1.0 + jnp.exp(-c)))
    o_ref[...] = jnp.dot(s.astype(BF16), w_ref[...].astype(BF16), preferred_element_type=F32) + b_ref[...]


def _ada(c_all, w_ada, b_ada):
    depth, d, n = w_ada.shape
    rows = c_all.shape[0]
    tn = 1536
    return pl.pallas_call(
        _ada_kernel,
        grid=(depth, n // tn),
        in_specs=[
            pl.BlockSpec((rows, d), lambda l, j: (0, 0)),
            pl.BlockSpec((None, d, tn), lambda l, j: (l, 0, j)),
            pl.BlockSpec((None, 1, tn), lambda l, j: (l, 0, j)),
        ],
        out_specs=pl.BlockSpec((None, rows, tn), lambda l, j: (l, 0, j)),
        out_shape=jax.ShapeDtypeStruct((depth, rows, n), F32),
        compiler_params=_cp(("parallel", "parallel")),
        name="ada_mod",
    )(c_all, w_ada, b_ada.reshape(depth, 1, n))


def _inproj_kernel(x_ref, shift_ref, scale_ref, g_ref, win_ref, gq_ref, wq_ref, gkv_ref, wuk_ref,
                   cos_ref, sin_ref,
                   sbq_ref, sbk_ref, sbv_ref, kb_ref, vb_ref, qcat_ref, ckv_ref, kpe_ref, kcat_ref):
    h = _rms(x_ref[...], g_ref[...]) * (1.0 + scale_ref[...]) + shift_ref[...]
    p = jnp.dot(h.astype(BF16), win_ref[...], preferred_element_type=F32)
    sbq_ref[...] = (p[:, :SB_WIDTH] * SB_SCALE).astype(BF16)
    k = p[:, SB_WIDTH:2 * SB_WIDTH]
    v = p[:, 2 * SB_WIDTH:3 * SB_WIDTH]
    sbk_ref[...] = k
    sbv_ref[...] = v
    kb_ref[...] = k.astype(BF16)
    vb_ref[...] = v.astype(BF16)

    c0 = 3 * SB_WIDTH
    qn = _rms(p[:, c0:c0 + Q_RANK], gq_ref[...])
    qq = jnp.dot(qn.astype(BF16), wq_ref[...], preferred_element_type=F32)
    qabs = jnp.dot(qq[:, :SB_WIDTH].astype(BF16), wuk_ref[...], preferred_element_type=F32)
    cos = cos_ref[...]
    sin = sin_ref[...]
    pe0 = MLA_HEADS * MLA_NOPE
    rh0 = pe0 + MLA_HEADS * LANES
    for hh in range(MLA_HEADS):
        pe = (qq[:, pe0 + LANES * hh:pe0 + LANES * (hh + 1)] * cos
              + qq[:, rh0 + LANES * hh:rh0 + LANES * (hh + 1)] * sin)
        qcat_ref[:, QCAT * hh:QCAT * hh + LANES] = qabs[:, LANES * hh:LANES * (hh + 1)].astype(BF16)
        qcat_ref[:, QCAT * hh + LANES:QCAT * (hh + 1)] = pe.astype(BF16)

    c1 = c0 + Q_RANK
    ckv = _rms(p[:, c1:c1 + KV_RANK], gkv_ref[...])
    ckv_ref[...] = ckv
    kg = p[:, c1 + KV_RANK:c1 + KV_RANK + LANES]
    kpe = kg * cos + pltpu.roll(kg, LANES - MLA_ROPE, 1) * sin
    kpe_ref[...] = kpe[:, :MLA_ROPE]
    kcat_ref[:, :KV_RANK] = ckv.astype(BF16)
    kcat_ref[:, KV_RANK:] = kpe.astype(BF16)


def _inproj(x, shift, scale, g, win, gq, wq, gkv, wuk, cos, sin, *, tm):
    rows, d = x.shape
    mod_rows = shift.shape[1]
    tiles_per_batch = cos.shape[0] // tm
    mod_map = lambda i: (i // tiles_per_batch, 0, 0)
    pos_map = lambda i: (i % tiles_per_batch, 0)
    row_map = lambda i: (i, 0)
    full = lambda i: (0, 0)
    wide = [(SB_WIDTH, BF16), (SB_WIDTH, F32), (SB_WIDTH, F32), (SB_WIDTH, BF16), (SB_WIDTH, BF16),
            (MLA_HEADS * QCAT, BF16), (KV_RANK, F32), (MLA_ROPE, F32), (QCAT, BF16)]
    return pl.pallas_call(
        _inproj_kernel,
        grid=(rows // tm,),
        in_specs=[
            pl.BlockSpec((tm, d), row_map),
            pl.BlockSpec((None, mod_rows, d), mod_map),
            pl.BlockSpec((None, mod_rows, d), mod_map),
            pl.BlockSpec((1, d), full),
            pl.BlockSpec(win.shape, full),
            pl.BlockSpec((1, Q_RANK), full),
            pl.BlockSpec(wq.shape, full),
            pl.BlockSpec((1, KV_RANK), full),
            pl.BlockSpec(wuk.shape, full),
            pl.BlockSpec((tm, LANES), pos_map),
            pl.BlockSpec((tm, LANES), pos_map),
        ],
        out_specs=[pl.BlockSpec((tm, w), row_map) for w, _ in wide],
        out_shape=[jax.ShapeDtypeStruct((rows, w), dt) for w, dt in wide],
        compiler_params=_cp(("parallel",)),
        name="inproj",
    )(x, shift, scale, g, win, gq, wq, gkv, wuk, cos, sin)


def _sb_kernel(q_ref, k_ref, v_ref, u_ref, o_ref, acc_ref, *, tq):
    qi = pl.program_id(2)
    q = q_ref[...].astype(F32)
    lane = lax.broadcasted_iota(jnp.int32, (tq, LANES), 1)
    qm = [jnp.where(lane < SB_HEAD_DIM, q, 0.0).astype(BF16), jnp.where(lane >= SB_HEAD_DIM, q, 0.0).astype(BF16)]
    vis = lax.broadcasted_iota(jnp.int32, (tq, tq), 1) < lax.broadcasted_iota(jnp.int32, (tq, tq), 0)
    u = u_ref[...]

    def block(kb, runs, diagonal):
        start = pl.multiple_of(kb * tq, tq)
        kblk = k_ref[pl.ds(start, tq), :]
        vblk = v_ref[pl.ds(start, tq), :]
        out = []
        for hh in range(2):
            z = _dot_nt(qm[hh], kblk)
            l1m = -_softplus(z)
            if diagonal:
                l1m = jnp.where(vis, l1m, 0.0)
            tail = jnp.dot(l1m.astype(BF16), u, preferred_element_type=F32) + runs[hh]
            a = jnp.exp(z + tail)
            if diagonal:
                a = jnp.where(vis, a, 0.0)
            acc_ref[hh] += jnp.dot(a.astype(BF16), vblk, preferred_element_type=F32)
            out.append(runs[hh] + jnp.sum(l1m, axis=1, keepdims=True))
        return tuple(out)

    acc_ref[...] = jnp.zeros_like(acc_ref)
    zero = jnp.zeros((tq, 1), F32)
    runs = block(qi, (zero, zero), True)

    def quad(i, r):
        for k in range(4):
            r = block(qi - 1 - 4 * i - k, r, False)
        return r

    runs = lax.fori_loop(0, qi // 4, quad, runs)
    rem = qi % 4
    runs = lax.cond(rem >= 2, lambda r: block(rem - 2, block(rem - 1, r, False), False), lambda r: r, runs)

    @pl.when(rem % 2 == 1)
    def _():
        block(0, runs, False)

    o_ref[...] = jnp.where(lane < SB_HEAD_DIM, acc_ref[0], acc_ref[1]).astype(o_ref.dtype)


def _sb_attention(q, k, v, u, *, tq):
    b, t, w = q.shape
    tk = tq
    kern = functools.partial(_sb_kernel, tq=tq)
    return pl.pallas_call(
        kern,
        grid=(b, w // LANES, t // tq),
        in_specs=[
            pl.BlockSpec((None, tq, LANES), lambda bi, hp, qi: (bi, qi, hp)),
            pl.BlockSpec((None, t, LANES), lambda bi, hp, qi: (bi, 0, hp)),
            pl.BlockSpec((None, t, LANES), lambda bi, hp, qi: (bi, 0, hp)),
            pl.BlockSpec((tk, tk), lambda bi, hp, qi: (0, 0)),
        ],
        out_specs=pl.BlockSpec((None, tq, LANES), lambda bi, hp, qi: (bi, qi, hp)),
        out_shape=jax.ShapeDtypeStruct((b, t, w), BF16),
        scratch_shapes=[pltpu.VMEM((2, tq, LANES), F32)],
        compiler_params=_cp(("parallel", "parallel", "parallel")),
        name="sb_attn",
    )(q, k, v, u)


def _mla_kernel(q_ref, k_ref, o_ref, *, tq, tk):
    qi = pl.program_id(1)
    rows = MLA_HEADS * tq
    qs = jnp.concatenate([q_ref[:, QCAT * hh:QCAT * (hh + 1)] for hh in range(MLA_HEADS)], axis=0)
    qpos = qi * tq + (lax.broadcasted_iota(jnp.int32, (rows, tk), 0) & (tq - 1))
    kcol = lax.broadcasted_iota(jnp.int32, (rows, tk), 1)
    nfull = (qi * tq) // tk

    def block(j, carry, masked):
        m, l, acc = carry
        start = pl.multiple_of(j * tk, tk)
        kblk = k_ref[pl.ds(start, tk), :]
        s = _dot_nt(qs, kblk)
        if masked:
            s = jnp.where((j * tk + kcol) <= qpos, s, NEG)
        m_new = jnp.maximum(m, jnp.max(s, axis=1, keepdims=True))
        alpha = jnp.exp((m - m_new) * MLA_SCALE)
        p = jnp.exp((s - m_new) * MLA_SCALE)
        l = alpha * l + jnp.sum(p, axis=1, keepdims=True)
        acc = alpha * acc + jnp.dot(p.astype(BF16), kblk[:, :KV_RANK], preferred_element_type=F32)
        return m_new, l, acc

    init = (jnp.full((rows, 1), NEG, F32), jnp.zeros((rows, 1), F32), jnp.zeros((rows, KV_RANK), F32))
    carry = lax.fori_loop(0, nfull, lambda j, c: block(j, c, False), init)
    _, l, acc = block(nfull, carry, True)
    out = acc / l
    for hh in range(MLA_HEADS):
        o_ref[:, KV_RANK * hh:KV_RANK * (hh + 1)] = out[tq * hh:tq * (hh + 1)].astype(o_ref.dtype)


def _mla_attention(qcat, kcat, *, tq, tk):
    b, t, _ = qcat.shape
    assert tk % tq == 0 and tq & (tq - 1) == 0
    kern = functools.partial(_mla_kernel, tq=tq, tk=tk)
    return pl.pallas_call(
        kern,
        grid=(b, t // tq),
        in_specs=[
            pl.BlockSpec((None, tq, MLA_HEADS * QCAT), lambda bi, qi: (bi, qi, 0)),
            pl.BlockSpec((None, t, QCAT), lambda bi, qi: (bi, 0, 0)),
        ],
        out_specs=pl.BlockSpec((None, tq, MLA_HEADS * KV_RANK), lambda bi, qi: (bi, qi, 0)),
        out_shape=jax.ShapeDtypeStruct((b, t, MLA_HEADS * KV_RANK), BF16),
        compiler_params=_cp(("parallel", "parallel")),
        name="mla_attn",
    )(qcat, kcat)


PAGES = 32
SEG = 256
QROWS = 16


def _sample_attn_kernel(pt_ref, qbd_ref, qcat_ref, knew_ref, vnew_ref, kcnew_ref, u_ref, *rest, past_len):
    pages = rest[:4 * PAGES]
    osb_ref, olat_ref = rest[4 * PAGES:4 * PAGES + 2]
    acc_sb, run_ref, m_ref, l_ref, acc_lat = rest[4 * PAGES + 2:]
    kp, vp, cp, pp = (pages[i * PAGES:(i + 1) * PAGES] for i in range(4))
    c = pl.program_id(1)
    qbd = qbd_ref[...]
    qcat = qcat_ref[...]

    @pl.when(c == 0)
    def _():
        kpos = jnp.full((QROWS, 1), past_len, jnp.int32)
        qpos = jnp.full((QROWS, 1), past_len, jnp.int32)
        z = jnp.sum(qbd.astype(F32) * knew_ref[...].astype(F32), axis=1, keepdims=True)
        vis = kpos < qpos
        l1m = jnp.where(vis, -_softplus(z), 0.0)
        a = jnp.where(vis, jnp.exp(z + l1m), 0.0)
        acc_sb[...] = a * vnew_ref[...].astype(F32)
        run_ref[...] = jnp.broadcast_to(l1m, (QROWS, LANES))
        kc = kcnew_ref[...].astype(F32)
        s = jnp.sum(qcat.astype(F32) * kc, axis=1, keepdims=True) * MLA_SCALE
        vis2 = kpos <= qpos
        m_ref[...] = jnp.broadcast_to(jnp.where(vis2, s, NEG), (QROWS, LANES))
        p = jnp.where(vis2, 1.0, 0.0)
        l_ref[...] = jnp.broadcast_to(p, (QROWS, LANES))
        acc_lat[...] = p * kc[:, :KV_RANK]

    nseg = PAGES * PAGE_SIZE // SEG
    kc = jnp.concatenate([r[...] for r in kp], axis=1).astype(BF16)
    z = jnp.dot(qbd, kc, preferred_element_type=F32)
    l1m = -_softplus(z)
    l4 = jnp.concatenate([l1m[:, SEG * s:SEG * (s + 1)] for s in range(nseg)], axis=0)
    hi, lo = _split_bf16(l4)
    u = u_ref[...]
    cum = jnp.dot(hi, u, preferred_element_type=F32) + jnp.dot(lo, u, preferred_element_type=F32)
    tot = jnp.sum(l4, axis=1, keepdims=True)
    carry = run_ref[:, :1]
    tails = [None] * nseg
    for s in reversed(range(nseg)):
        tails[s] = cum[QROWS * s:QROWS * (s + 1)] + carry
        carry = carry + tot[QROWS * s:QROWS * (s + 1)]
    run_ref[...] = jnp.broadcast_to(carry, (QROWS, LANES))
    a = jnp.exp(z + jnp.concatenate(tails, axis=1))
    vc = jnp.concatenate([r[...] for r in vp], axis=1).astype(BF16)
    acc_sb[...] += _dot_nt(a.astype(BF16), vc)

    ck = jnp.concatenate([r[...] for r in cp], axis=0).astype(BF16)
    pk = jnp.concatenate([r[...] for r in pp], axis=1).astype(BF16)
    s = (_dot_nt(qcat[:, :KV_RANK], ck)
         + jnp.dot(qcat[:, KV_RANK:KV_RANK + MLA_ROPE], pk, preferred_element_type=F32)) * MLA_SCALE
    m_old = m_ref[:, :1]
    m_new = jnp.maximum(m_old, jnp.max(s, axis=1, keepdims=True))
    alpha = jnp.exp(m_old - m_new)
    p = jnp.exp(s - m_new)
    l_new = alpha * l_ref[:, :1] + jnp.sum(p, axis=1, keepdims=True)
    acc_lat[...] = alpha * acc_lat[...] + jnp.dot(p.astype(BF16), ck, preferred_element_type=F32)
    m_ref[...] = jnp.broadcast_to(m_new, (QROWS, LANES))
    l_ref[...] = jnp.broadcast_to(l_new, (QROWS, LANES))

    @pl.when(c == pl.num_programs(1) - 1)
    def _():
        acc = acc_sb[...]
        row = lax.broadcasted_iota(jnp.int32, acc.shape, 0)
        head = lax.broadcasted_iota(jnp.int32, acc.shape, 1) // SB_HEAD_DIM
        osb_ref[...] = jnp.sum(jnp.where(row == head, acc, 0.0), axis=0, keepdims=True).astype(osb_ref.dtype)
        olat_ref[...] = (acc_lat[...] / l_ref[...])[:MLA_HEADS].astype(olat_ref.dtype)


def _sample_attention(layer, page_table, qbd, qcat, knew, vnew, kcnew, u, cache_k, cache_v, cache_c, cache_p):
    nb, n_pages = page_table.shape
    nchunk = n_pages // PAGES
    past_len = n_pages * PAGE_SIZE

    def page_spec(shape, i):
        def imap(b, c, pt):
            return (layer, pt[b, (nchunk - 1 - c) * PAGES + i], 0, 0)
        return pl.BlockSpec((None, None) + shape, imap)

    per_b = lambda b, c, pt: (b, 0, 0)
    in_specs = [
        pl.BlockSpec((None, QROWS, SB_WIDTH), per_b),
        pl.BlockSpec((None, QROWS, QCAT), per_b),
        pl.BlockSpec((None, 1, SB_WIDTH), per_b),
        pl.BlockSpec((None, 1, SB_WIDTH), per_b),
        pl.BlockSpec((None, 1, QCAT), per_b),
        pl.BlockSpec((SEG, SEG), lambda b, c, pt: (0, 0)),
    ]
    operands = [qbd, qcat, knew, vnew, kcnew, u]
    for cache in (cache_k, cache_v, cache_c, cache_p):
        for i in range(PAGES):
            in_specs.append(page_spec(cache.shape[2:], i))
            operands.append(cache)
    grid_spec = pltpu.PrefetchScalarGridSpec(
        num_scalar_prefetch=1,
        grid=(nb, nchunk),
        in_specs=in_specs,
        out_specs=[
            pl.BlockSpec((None, 1, SB_WIDTH), per_b),
            pl.BlockSpec((None, MLA_HEADS, KV_RANK), per_b),
        ],
        scratch_shapes=[
            pltpu.VMEM((QROWS, SB_WIDTH), F32),
            pltpu.VMEM((QROWS, LANES), F32),
            pltpu.VMEM((QROWS, LANES), F32),
            pltpu.VMEM((QROWS, LANES), F32),
            pltpu.VMEM((QROWS, KV_RANK), F32),
        ],
    )
    return pl.pallas_call(
        functools.partial(_sample_attn_kernel, past_len=past_len),
        grid_spec=grid_spec,
        out_shape=[
            jax.ShapeDtypeStruct((nb, 1, SB_WIDTH), BF16),
            jax.ShapeDtypeStruct((nb, MLA_HEADS, KV_RANK), BF16),
        ],
        compiler_params=_cp(("parallel", "arbitrary")),
        name="sample_attn",
    )(page_table, *operands)


def _outproj_kernel(*refs, moe, emit_t):
    (osb_ref, olat_ref, x_ref, gate_ref, wuv_ref, wout_ref, g_ref, shift_ref, scale_ref) = refs[:9]
    if moe and emit_t:
        wr_ref, br_ref, xo_ref, h_ref, comb_ref, combt_ref = refs[9:]
    elif moe:
        wr_ref, br_ref, xo_ref, h_ref, comb_ref = refs[9:]
    else:
        xo_ref, h_ref = refs[9:]
    omla = jnp.dot(olat_ref[...], wuv_ref[...], preferred_element_type=F32)
    o = jnp.concatenate([osb_ref[...], omla.astype(BF16)], axis=1)
    x = x_ref[...] + gate_ref[...] * jnp.dot(o, wout_ref[...], preferred_element_type=F32)
    xo_ref[...] = x
    h = _rms(x, g_ref[...]) * (1.0 + scale_ref[...]) + shift_ref[...]
    h_ref[...] = h.astype(BF16)
    if moe:
        logits = jnp.dot(h.astype(BF16), wr_ref[...].astype(BF16), preferred_element_type=F32) + br_ref[...]
        lane = lax.broadcasted_iota(jnp.int32, logits.shape, 1).astype(F32)
        lg = jnp.where(lane < N_EXPERTS, logits, -jnp.inf)
        m1 = jnp.max(lg, axis=1, keepdims=True)
        i1 = jnp.min(jnp.where(lg == m1, lane, float(LANES)), axis=1, keepdims=True)
        lg2 = jnp.where(lane == i1, -jnp.inf, lg)
        m2 = jnp.max(lg2, axis=1, keepdims=True)
        i2 = jnp.min(jnp.where(lg2 == m2, lane, float(LANES)), axis=1, keepdims=True)
        e = jnp.exp(m2 - m1)
        g1 = 1.0 / (1.0 + e)
        g2 = e / (1.0 + e)
        comb = jnp.where(lane == i1, g1, 0.0) + jnp.where(lane == i2, g2, 0.0)
        comb_ref[...] = comb
        if emit_t:
            combt_ref[...] = comb.T[:N_EXPERTS]


def _outproj(osb, olat, x, gate, wuv, wout, g, shift, scale, router, *, tm, tiles_per_batch, emit_t=False):
    rows, d = x.shape
    mod_rows = gate.shape[1]
    moe = router is not None
    mod_map = lambda i: (i // tiles_per_batch, 0, 0)
    row_map = lambda i: (i, 0)
    full = lambda i: (0, 0)
    in_specs = [
        pl.BlockSpec((tm, SB_WIDTH), row_map),
        pl.BlockSpec((tm, MLA_HEADS * KV_RANK), row_map),
        pl.BlockSpec((tm, d), row_map),
        pl.BlockSpec((None, mod_rows, d), mod_map),
        pl.BlockSpec(wuv.shape, full),
        pl.BlockSpec(wout.shape, full),
        pl.BlockSpec((1, d), full),
        pl.BlockSpec((None, mod_rows, d), mod_map),
        pl.BlockSpec((None, mod_rows, d), mod_map),
    ]
    operands = [osb, olat, x, gate, wuv, wout, g, shift, scale]
    out_specs = [pl.BlockSpec((tm, d), row_map), pl.BlockSpec((tm, d), row_map)]
    out_shape = [jax.ShapeDtypeStruct((rows, d), F32), jax.ShapeDtypeStruct((rows, d), BF16)]
    if moe:
        in_specs += [pl.BlockSpec((d, LANES), full), pl.BlockSpec((1, LANES), full)]
        operands += list(router)
        out_specs.append(pl.BlockSpec((tm, LANES), row_map))
        out_shape.append(jax.ShapeDtypeStruct((rows, LANES), F32))
        if emit_t:
            out_specs.append(pl.BlockSpec((N_EXPERTS, tm), lambda i: (0, i)))
            out_shape.append(jax.ShapeDtypeStruct((N_EXPERTS, rows), F32))
    return pl.pallas_call(
        functools.partial(_outproj_kernel, moe=moe, emit_t=moe and emit_t),
        grid=(rows // tm,),
        in_specs=in_specs,
        out_specs=out_specs,
        out_shape=out_shape,
        compiler_params=_cp(("parallel",)),
        name="outproj",
    )(*operands)


def _ffn_kernel(*refs, moe):
    if moe:
        h_ref, wg_ref, wu_ref, wd_ref, x_ref, gate_ref, comb_ref, o_ref, acc_ref = refs
    else:
        h_ref, wg_ref, wu_ref, wd_ref, x_ref, gate_ref, o_ref, acc_ref = refs
    e = pl.program_id(1)
    j = pl.program_id(2)

    @pl.when((e == 0) & (j == 0))
    def _():
        acc_ref[...] = jnp.zeros_like(acc_ref)

    h = h_ref[...]
    gt = jnp.dot(h, wg_ref[...], preferred_element_type=F32)
    up = jnp.dot(h, wu_ref[...], preferred_element_type=F32)
    act = gt * (1.0 / (1.0 + jnp.exp(-gt))) * up
    y = jnp.dot(act.astype(BF16), wd_ref[...], preferred_element_type=F32)
    if moe:
        comb = comb_ref[...]
        lane = lax.broadcasted_iota(jnp.int32, comb.shape, 1)
        y = jnp.sum(jnp.where(lane == e, comb, 0.0), axis=1, keepdims=True) * y
    acc_ref[...] += y

    @pl.when((e == pl.num_programs(1) - 1) & (j == pl.num_programs(2) - 1))
    def _():
        o_ref[...] = x_ref[...] + gate_ref[...] * acc_ref[...]


def _ffn(h, wg, wu, wd, li, x, gate, comb, *, tm, tf, tiles_per_batch):
    rows, d = x.shape
    _, ne, _, f = wg.shape
    mod_rows = gate.shape[1]
    moe = comb is not None
    row_map = lambda i, e, j: (i, 0)
    in_specs = [
        pl.BlockSpec((tm, d), row_map),
        pl.BlockSpec((None, None, d, tf), lambda i, e, j: (li, e, 0, j)),
        pl.BlockSpec((None, None, d, tf), lambda i, e, j: (li, e, 0, j)),
        pl.BlockSpec((None, None, tf, d), lambda i, e, j: (li, e, j, 0)),
        pl.BlockSpec((tm, d), row_map),
        pl.BlockSpec((None, mod_rows, d), lambda i, e, j: (i // tiles_per_batch, 0, 0)),
    ]
    operands = [h, wg, wu, wd, x, gate]
    if moe:
        in_specs.append(pl.BlockSpec((tm, LANES), row_map))
        operands.append(comb)
    return pl.pallas_call(
        functools.partial(_ffn_kernel, moe=moe),
        grid=(rows // tm, ne, f // tf),
        in_specs=in_specs,
        out_specs=pl.BlockSpec((tm, d), row_map),
        out_shape=jax.ShapeDtypeStruct((rows, d), F32),
        scratch_shapes=[pltpu.VMEM((tm, d), F32)],
        compiler_params=_cp(("parallel", "arbitrary", "arbitrary")),
        name="ffn",
    )(*operands)


MOE_TILE = 1024
MOE_CHUNK = 288


def _moe_kernel(h_ref, combt_ref, comb_ref, before_ref, after_ref, wg_ref, wu_ref, wd_ref, x_ref, gate_ref,
                o_ref, xc_ref, gc_ref, yc_ref, acc_ref, rankrow_ref, rankcol_ref, nch_ref):
    e = pl.program_id(1)
    j = pl.program_id(2)
    last_j = j == pl.num_programs(2) - 1
    ts, d = h_ref.shape
    ch = MOE_CHUNK

    @pl.when((e == 0) & (j == 0))
    def _():
        acc_ref[...] = jnp.zeros_like(acc_ref)
        wt = combt_ref[...]
        cnt_before = jnp.dot(jnp.where(wt > 0.0, 1.0, 0.0).astype(BF16), before_ref[...],
                             preferred_element_type=F32)
        rankrow_ref[...] = jnp.where(wt > 0.0, cnt_before, -1.0)
        w = comb_ref[...]
        cnt_after = jnp.dot(after_ref[...], jnp.where(w > 0.0, 1.0, 0.0).astype(BF16),
                            preferred_element_type=F32)
        rankcol_ref[...] = jnp.where(w > 0.0, cnt_after, -1.0)

    @pl.when(j == 0)
    def _():
        rank_row = rankrow_ref[pl.ds(e, 1), :]
        nch = (jnp.sum(jnp.where(rank_row >= 0.0, 1.0, 0.0)).astype(jnp.int32) + (ch - 1)) // ch
        nch_ref[0] = nch
        h = h_ref[...]
        chi, clo = _split_bf16(comb_ref[...])
        lane = lax.broadcasted_iota(jnp.int32, (ch, LANES), 1)
        slot = lax.broadcasted_iota(jnp.int32, (ch, ts), 0).astype(F32)

        def gather(c, carry):
            base = (c * ch).astype(F32)
            p = jnp.where(rank_row == slot + base, 1.0, 0.0).astype(BF16)
            xc_ref[c] = jnp.dot(p, h, preferred_element_type=F32).astype(BF16)
            gall = jnp.dot(p, chi, preferred_element_type=F32) + jnp.dot(p, clo, preferred_element_type=F32)
            g = jnp.sum(jnp.where(lane == e, gall, 0.0), axis=1, keepdims=True)
            gc_ref[c] = jnp.broadcast_to(g, (ch, LANES))
            yc_ref[c] = jnp.zeros((ch, d), F32)
            return carry

        lax.fori_loop(0, nch, gather, 0)

    def expert(c, carry):
        xcc = xc_ref[c]
        gt = jnp.dot(xcc, wg_ref[...], preferred_element_type=F32)
        up = jnp.dot(xcc, wu_ref[...], preferred_element_type=F32)
        act = gt * (1.0 / (1.0 + jnp.exp(-gt))) * up
        yc_ref[c] += jnp.dot(act.astype(BF16), wd_ref[...], preferred_element_type=F32)
        return carry

    lax.fori_loop(0, nch_ref[0], expert, 0)

    @pl.when(last_j)
    def _():
        lane = lax.broadcasted_iota(jnp.int32, (ts, LANES), 1)
        rank = jnp.sum(jnp.where(lane == e, rankcol_ref[...], 0.0), axis=1, keepdims=True)
        slot = lax.broadcasted_iota(jnp.int32, (ts, ch), 1).astype(F32)

        def scatter(c, carry):
            base = (c * ch).astype(F32)
            y = (yc_ref[c] * gc_ref[c][:, :1]).astype(BF16)
            pt = jnp.where(rank == slot + base, 1.0, 0.0).astype(BF16)
            acc_ref[...] += jnp.dot(pt, y, preferred_element_type=F32)
            return carry

        lax.fori_loop(0, nch_ref[0], scatter, 0)

    @pl.when(last_j & (e == pl.num_programs(1) - 1))
    def _():
        o_ref[...] = x_ref[...] + gate_ref[...] * acc_ref[...]


def _moe(h, combt, comb, wg, wu, wd, li, x, gate, *, ts, tf, tiles_per_batch):
    rows, d = x.shape
    _, ne, _, f = wg.shape
    tri = jnp.triu(jnp.ones((ts, ts), F32), k=1).astype(BF16)
    row_map = lambda i, e, j: (i, 0)
    full = lambda i, e, j: (0, 0)
    return pl.pallas_call(
        _moe_kernel,
        grid=(rows // ts, ne, f // tf),
        in_specs=[
            pl.BlockSpec((ts, d), row_map),
            pl.BlockSpec((8, ts), lambda i, e, j: (0, i)),
            pl.BlockSpec((ts, LANES), row_map),
            pl.BlockSpec((ts, ts), full, pipeline_mode=pl.Buffered(1)),
            pl.BlockSpec((ts, ts), full, pipeline_mode=pl.Buffered(1)),
            pl.BlockSpec((None, None, d, tf), lambda i, e, j: (li, e, 0, j)),
            pl.BlockSpec((None, None, d, tf), lambda i, e, j: (li, e, 0, j)),
            pl.BlockSpec((None, None, tf, d), lambda i, e, j: (li, e, j, 0)),
            pl.BlockSpec((ts, d), row_map, pipeline_mode=pl.Buffered(1)),
            pl.BlockSpec((None, 1, d), lambda i, e, j: (i // tiles_per_batch, 0, 0)),
        ],
        out_specs=pl.BlockSpec((ts, d), row_map),
        out_shape=jax.ShapeDtypeStruct((rows, d), F32),
        scratch_shapes=[
            pltpu.VMEM((pl.cdiv(ts, MOE_CHUNK), MOE_CHUNK, d), BF16),
            pltpu.VMEM((pl.cdiv(ts, MOE_CHUNK), MOE_CHUNK, LANES), F32),
            pltpu.VMEM((pl.cdiv(ts, MOE_CHUNK), MOE_CHUNK, d), F32),
            pltpu.VMEM((ts, d), F32),
            pltpu.VMEM((8, ts), F32),
            pltpu.VMEM((ts, LANES), F32),
            pltpu.SMEM((1,), jnp.int32),
        ],
        compiler_params=_cp(("parallel", "arbitrary", "arbitrary")),
        name="moe",
    )(h, combt, comb, tri, tri.T, wg, wu, wd, x, gate)


def _norm_kernel(x_ref, g_ref, o_ref):
    o_ref[...] = _rms(x_ref[...], g_ref[...])


def _final_norm(x, g, *, tm):
    rows, d = x.shape
    return pl.pallas_call(
        _norm_kernel,
        grid=(rows // tm,),
        in_specs=[pl.BlockSpec((tm, d), lambda i: (i, 0)), pl.BlockSpec((1, d), lambda i: (0, 0))],
        out_specs=pl.BlockSpec((tm, d), lambda i: (i, 0)),
        out_shape=jax.ShapeDtypeStruct((rows, d), F32),
        compiler_params=_cp(("parallel",)),
        name="final_norm",
    )(x, g)


def _rotate_half_cols(w):
    half = MLA_ROPE // 2
    return jnp.concatenate([-w[..., half:], w[..., :half]], axis=-1)


def _prep_layer_weights(w_in, w_q_up, w_uk, w_uv, w_out):
    d = w_in.shape[0]
    c_kpe = 3 * SB_WIDTH + Q_RANK + KV_RANK
    kpe_w = w_in[:, c_kpe:c_kpe + MLA_ROPE]
    win = jnp.concatenate(
        [w_in, _rotate_half_cols(kpe_w), jnp.zeros((d, IN_EXT - c_kpe - 2 * MLA_ROPE), F32)], axis=1).astype(BF16)

    wq = w_q_up.reshape(Q_RANK, MLA_HEADS, MLA_NOPE + MLA_ROPE)
    nope = wq[:, :, :MLA_NOPE].reshape(Q_RANK, MLA_HEADS * MLA_NOPE)
    pe = wq[:, :, MLA_NOPE:]
    pad = jnp.zeros((Q_RANK, MLA_HEADS, LANES - MLA_ROPE), F32)
    pe_ext = jnp.concatenate([pe, pad], axis=2).reshape(Q_RANK, MLA_HEADS * LANES)
    rh_ext = jnp.concatenate([_rotate_half_cols(pe), pad], axis=2).reshape(Q_RANK, MLA_HEADS * LANES)
    wq_all = jnp.concatenate([nope, pe_ext, rh_ext], axis=1).astype(BF16)

    eye = jnp.eye(MLA_HEADS, dtype=F32)
    wuk_bd = jnp.einsum('chn,hg->hngc', w_uk, eye).reshape(MLA_HEADS * MLA_NOPE, MLA_HEADS * KV_RANK).astype(BF16)
    wuv_bd = jnp.einsum('chv,hg->hcgv', w_uv, eye).reshape(MLA_HEADS * KV_RANK, MLA_HEADS * MLA_V).astype(BF16)
    return win, wq_all, wuk_bd, wuv_bd, w_out.astype(BF16)


def _rope_tables(pos):
    inv = ROPE_THETA ** (-jnp.arange(0, MLA_ROPE, 2, dtype=F32) / MLA_ROPE)
    ang = pos.astype(F32)[:, None] * inv[None, :]
    pad = jnp.zeros((pos.shape[0], LANES - MLA_ROPE), F32)
    cos = jnp.concatenate([jnp.cos(ang), jnp.cos(ang), pad], axis=1)
    sin = jnp.concatenate([jnp.sin(ang), jnp.sin(ang), pad], axis=1)
    return cos, sin


def _row_tile(rows, target):
    return min(rows, target)


def kernel(x_prompt, x_sample, c_prompt, c_sample, cache_sb_k, cache_sb_v, cache_mla_ckv, cache_mla_kpe, page_table, w_ada, b_ada, g_mix, g_ffn, w_in, g_q, w_q_up, g_kv, w_uk, w_uv, w_out, w_gate_d, w_up_d, w_down_d, w_router, b_router, w_gate_e, w_up_e, w_down_e, g_final):
    nbp, t, d = x_prompt.shape
    nbs = x_sample.shape[0]
    depth = w_in.shape[0]
    n_pool = cache_sb_k.shape[1]
    past_len = page_table.shape[1] * PAGE_SIZE
    assert x_sample.shape[1] == 1 and page_table.shape[1] % PAGES == 0

    xp = x_prompt.reshape(nbp * t, d)
    xs = x_sample.reshape(nbs, d)
    tm_p = _row_tile(t, 512)
    tpb_p = t // tm_p
    tq_sb = _row_tile(t, 256)
    tq_mla = _row_tile(t, 128)

    mod = _ada(jnp.concatenate([c_prompt, c_sample], axis=0), w_ada, b_ada)
    mod = mod.reshape(depth, nbp + nbs, 6, d)
    mod_p = mod[:, :nbp].transpose(0, 2, 1, 3).reshape(depth, 6, nbp, 1, d)
    mod_s = mod[:, nbp:].transpose(0, 2, 1, 3).reshape(depth, 6, 1, nbs, d)

    cos_p, sin_p = _rope_tables(jnp.arange(t))
    cos_s, sin_s = _rope_tables(jnp.full((nbs,), past_len))
    u_sb = jnp.tril(jnp.ones((tq_sb, tq_sb), F32)).astype(BF16)
    u_seg = jnp.tril(jnp.ones((SEG, SEG), F32)).astype(BF16)

    ck = jnp.transpose(cache_sb_k, (0, 1, 3, 4, 2)).reshape(depth, n_pool, SB_WIDTH, PAGE_SIZE)
    cv = jnp.transpose(cache_sb_v, (0, 1, 3, 4, 2)).reshape(depth, n_pool, SB_WIDTH, PAGE_SIZE)
    cpe = jnp.transpose(cache_mla_kpe, (0, 1, 3, 2))
    head_of_lane = jnp.arange(SB_WIDTH) // SB_HEAD_DIM
    head_mask = (jnp.arange(QROWS)[:, None] == head_of_lane[None, :])

    ffn_dense = tuple(w.astype(BF16).reshape((w.shape[0], 1) + w.shape[1:]) for w in (w_gate_d, w_up_d, w_down_d))
    ffn_experts = tuple(w.astype(BF16) for w in (w_gate_e, w_up_e, w_down_e))

    new_p = ([], [], [], [])
    new_s = ([], [], [], [])
    for layer in range(depth):
        win, wq_all, wuk_bd, wuv_bd, wout = _prep_layer_weights(
            w_in[layer], w_q_up[layer], w_uk[layer], w_uv[layer], w_out[layer])
        gm = g_mix[layer].reshape(1, d)
        gf = g_ffn[layer].reshape(1, d)
        gq = g_q[layer].reshape(1, Q_RANK)
        gkv = g_kv[layer].reshape(1, KV_RANK)
        i = layer // 2
        if layer % 2 == 0:
            router = None
            wg, wu, wd = ffn_dense
        else:
            wr = jnp.concatenate([w_router[i], jnp.zeros((d, LANES - N_EXPERTS), F32)], axis=1)
            br = jnp.concatenate([b_router[i], jnp.zeros((LANES - N_EXPERTS,), F32)]).reshape(1, LANES)
            router = (wr, br)
            wg, wu, wd = ffn_experts

        mp = mod_p[layer]
        sbq, sbk, sbv, kb, vb, qcat, ckv, kpe, kcat = _inproj(
            xp, mp[0], mp[1], gm, win, gq, wq_all, gkv, wuk_bd, cos_p, sin_p, tm=tm_p)
        o_sb = _sb_attention(sbq.reshape(nbp, t, SB_WIDTH), kb.reshape(nbp, t, SB_WIDTH),
                             vb.reshape(nbp, t, SB_WIDTH), u_sb, tq=tq_sb)
        o_lat = _mla_attention(qcat.reshape(nbp, t, MLA_HEADS * QCAT), kcat.reshape(nbp, t, QCAT),
                               tq=tq_mla, tk=min(t, 256))
        res = _outproj(o_sb.reshape(nbp * t, SB_WIDTH), o_lat.reshape(nbp * t, MLA_HEADS * KV_RANK), xp, mp[2],
                       wuv_bd, wout, gf, mp[3], mp[4], router, tm=tm_p, tiles_per_batch=tpb_p, emit_t=True)
        xp, hp = res[0], res[1]
        if router:
            ts_moe = min(MOE_TILE, t)
            xp = _moe(hp, res[3], res[2], wg, wu, wd, i, xp, mp[5], ts=ts_moe, tf=D_FF // 2,
                      tiles_per_batch=t // ts_moe)
        else:
            xp = _ffn(hp, wg, wu, wd, i, xp, mp[5], None, tm=tm_p, tf=D_FF // 2, tiles_per_batch=tpb_p)
        for lst, arr in zip(new_p, (sbk, sbv, ckv, kpe)):
            lst.append(arr)

        ms = mod_s[layer]
        sbq, sbk, sbv, kb, vb, qcat, ckv, kpe, kcat = _inproj(
            xs, ms[0], ms[1], gm, win, gq, wq_all, gkv, wuk_bd, cos_s, sin_s, tm=nbs)
        qbd = jnp.where(head_mask[None], sbq[:, None, :], jnp.zeros((), BF16))
        qc = qcat.reshape(nbs, MLA_HEADS, QCAT)
        qc = jnp.concatenate([qc, jnp.zeros((nbs, QROWS - MLA_HEADS, QCAT), BF16)], axis=1)
        o_sb, o_lat = _sample_attention(
            layer, page_table, qbd, qc, kb.reshape(nbs, 1, SB_WIDTH), vb.reshape(nbs, 1, SB_WIDTH),
            kcat.reshape(nbs, 1, QCAT), u_seg, ck, cv, cache_mla_ckv, cpe)
        res = _outproj(o_sb.reshape(nbs, SB_WIDTH), o_lat.reshape(nbs, MLA_HEADS * KV_RANK), xs, ms[2],
                       wuv_bd, wout, gf, ms[3], ms[4], router, tm=nbs, tiles_per_batch=1)
        xs, hs = res[0], res[1]
        xs = _ffn(hs, wg, wu, wd, i, xs, ms[5], res[2] if router else None, tm=nbs, tf=D_FF // 2,
                  tiles_per_batch=1)
        for lst, arr in zip(new_s, (sbk, sbv, ckv, kpe)):
            lst.append(arr)

    gfin = g_final.reshape(1, d)
    y_prompt = _final_norm(xp, gfin, tm=tm_p).reshape(nbp, t, d)
    y_sample = _final_norm(xs, gfin, tm=nbs).reshape(nbs, 1, d)

    def stack(lst, nb, tt, tail):
        return jnp.stack(lst, 0).reshape((depth, nb, tt) + tail)

    hd = (SB_HEADS, SB_HEAD_DIM)
    return (y_prompt, y_sample,
            stack(new_p[0], nbp, t, hd), stack(new_p[1], nbp, t, hd),
            stack(new_p[2], nbp, t, (KV_RANK,)), stack(new_p[3], nbp, t, (MLA_ROPE,)),
            stack(new_s[0], nbs, 1, hd), stack(new_s[1], nbs, 1, hd),
            stack(new_s[2], nbs, 1, (KV_RANK,)), stack(new_s[3], nbs, 1, (MLA_ROPE,)))
```
